```python
import math
import jax
import jax.numpy as jnp
from jax import lax
import numpy as np

D_MODEL = 1024
BATCH = 2
SEQ = 8192
DEPTH = 2
DEC_BATCH = 128
DEC_SEQ = 4
PAST_LEN = 16384
PAGE_SIZE = 128

N_A_LAYERS = DEPTH // 2
N_B_LAYERS = DEPTH - N_A_LAYERS
D_RNN = D_MODEL
RNN_BLOCKS = 8
RNN_BLOCK_W = D_RNN // RNN_BLOCKS
CONV_W = 4
RG_C = 8.0
MLA_HEADS = 16
QK_NOPE = 64
QK_ROPE = 32
V_DIM = 64
KV_LORA = 256
Q_LORA = 384
ROPE_BASE = 10000.0
Q_BLOCK = 128
MLA_SCALE = (QK_NOPE + QK_ROPE) ** -0.5
N_MEM = 256
MEM_HEADS = 4
MEM_HD = 128
MEM_W = MEM_HEADS * MEM_HD
PEER_HEADS = 8
N_KEYS = 128
N_EXPERTS = N_KEYS * N_KEYS
PEER_QK = 128
PEER_HALF = PEER_QK // 2
PEER_TOPK = 16
PEER_CHUNK = 128
NORM_EPS = 1e-6

kernel_name = 'yoco_rglru_mla_peer_step'


def rms_norm(x, g):
    xf = x.astype(jnp.float32)
    y = xf * lax.rsqrt(jnp.mean(xf * xf, axis=-1, keepdims=True) + NORM_EPS)
    return (y * g.astype(jnp.float32)).astype(x.dtype)


def rope(x, pos):
    half = x.shape[-1] // 2
    freqs = ROPE_BASE ** (-jnp.arange(half, dtype=jnp.float32) / half)
    ang = pos.astype(jnp.float32)[:, None] * freqs
    ang = ang.reshape((1, ang.shape[0]) + (1,) * (x.ndim - 3) + (half,))
    cos, sin = jnp.cos(ang), jnp.sin(ang)
    xf = x.astype(jnp.float32)
    x1, x2 = xf[..., :half], xf[..., half:]
    return jnp.concatenate([x1 * cos - x2 * sin, x1 * sin + x2 * cos], axis=-1).astype(x.dtype)


def rg_lru_branch(xb, conv_state, h0, conv_w, conv_b, wr, br, wi, bi, lam):
    B, T, C = xb.shape
    ext = jnp.concatenate([conv_state.astype(xb.dtype), xb], axis=1)
    conv = conv_b + sum(ext[:, k:k + T] * conv_w[k] for k in range(CONV_W))
    new_conv = ext[:, T:]
    blocks = conv.reshape(B, T, RNN_BLOCKS, RNN_BLOCK_W)
    r = jax.nn.sigmoid((jnp.einsum('btnk,nkj->btnj', blocks, wr).reshape(B, T, C) + br).astype(jnp.float32))
    i = jax.nn.sigmoid((jnp.einsum('btnk,nkj->btnj', blocks, wi).reshape(B, T, C) + bi).astype(jnp.float32))
    log_a = -RG_C * r * jax.nn.softplus(-lam.astype(jnp.float32))
    a = jnp.exp(log_a)
    b = jnp.sqrt(-jnp.expm1(2.0 * log_a)) * (i * conv.astype(jnp.float32))
    b = b.at[:, 0].add(a[:, 0] * h0.astype(jnp.float32))

    def combine(e1, e2):
        return (e1[0] * e2[0], e2[0] * e1[1] + e2[1])

    _, h = lax.associative_scan(combine, (a, b), axis=1)
    return h.astype(xb.dtype), new_conv, h[:, -1].astype(xb.dtype)


def mem_attention(q, mk, mv):
    B, T, _ = q.shape
    q = q.reshape(B, T, MEM_HEADS, MEM_HD)
    s = jnp.einsum('bthd,bmhd->bhtm', q, mk).astype(jnp.float32) * (MEM_HD ** -0.5)
    p = jax.nn.softmax(s, axis=-1).astype(mv.dtype)
    return jnp.einsum('bhtm,bmhd->bthd', p, mv).reshape(B, T, MEM_W)


def mla_prompt_attention(q_nope, q_rope, c_kv, k_rope, w_uk, w_uv):
    B, T, H, _ = q_nope.shape
    k_nope = jnp.einsum('bsc,chn->bshn', c_kv, w_uk)
    v = jnp.einsum('bsc,chv->bshv', c_kv, w_uv)
    q = jnp.concatenate([q_nope, q_rope], axis=-1)
    k = jnp.concatenate([k_nope, jnp.broadcast_to(k_rope[:, :, None, :], (B, T, H, QK_ROPE))], axis=-1)
    outs = []
    for start in range(0, T, Q_BLOCK):
        end = min(start + Q_BLOCK, T)
        s = jnp.einsum('bthd,bshd->bhts', q[:, start:end], k[:, :end]).astype(jnp.float32) * MLA_SCALE
        mask = jnp.arange(end)[None, :] <= jnp.arange(start, end)[:, None]
        p = jax.nn.softmax(jnp.where(mask, s, -jnp.inf), axis=-1).astype(v.dtype)
        outs.append(jnp.einsum('bhts,bshv->bthv', p, v[:, :end]))
    return jnp.concatenate(outs, axis=1).reshape(B, T, H * V_DIM)


def mla_sample_attention(q_nope, q_rope, c_past, kr_past, c_new, kr_new, w_uk, w_uv):
    B, T, H, _ = q_nope.shape
    P = c_past.shape[1]
    q_lat = jnp.einsum('bthn,chn->bthc', q_nope, w_uk)
    s_past = jnp.einsum('bthc,bsc->bhts', q_lat, c_past) + jnp.einsum('bthr,bsr->bhts', q_rope, kr_past)
    s_new = jnp.einsum('bthc,bsc->bhts', q_lat, c_new) + jnp.einsum('bthr,bsr->bhts', q_rope, kr_new)
    causal = jnp.tril(jnp.ones((T, T), dtype=bool))
    s = jnp.concatenate([s_past.astype(jnp.float32), jnp.where(causal, s_new.astype(jnp.float32), -jnp.inf)], axis=-1) * MLA_SCALE
    p = jax.nn.softmax(s, axis=-1).astype(c_new.dtype)
    o = jnp.einsum('bhts,bsc->bthc', p[..., :P], c_past) + jnp.einsum('bhts,bsc->bthc', p[..., P:], c_new)
    return jnp.einsum('bthc,chv->bthv', o, w_uv).reshape(B, T, H * V_DIM)


def peer_ffn(x, wq, sub_keys, u, v):
    shp = x.shape
    tok = x.reshape(-1, shp[-1])
    n = tok.shape[0]
    tok = jnp.pad(tok, ((0, (-n) % PEER_CHUNK), (0, 0)))

    def chunk(xc):
        c = xc.shape[0]
        q = (xc @ wq).reshape(c, PEER_HEADS, 2, PEER_HALF)
        s = jnp.einsum('chpd,pkd->chpk', q, sub_keys)
        ts, ti = lax.top_k(s, PEER_TOPK)
        cand_s = (ts[:, :, 0, :, None] + ts[:, :, 1, None, :]).reshape(c, PEER_HEADS, PEER_TOPK * PEER_TOPK)
        cand_i = (ti[:, :, 0, :, None] * N_KEYS + ti[:, :, 1, None, :]).reshape(c, PEER_HEADS, PEER_TOPK * PEER_TOPK)
        best_s, best_j = lax.top_k(cand_s, PEER_TOPK)
        ids = jnp.take_along_axis(cand_i, best_j, axis=-1)
        g = jax.nn.softmax(best_s.astype(jnp.float32), axis=-1)
        act = jax.nn.gelu(jnp.einsum('chkd,cd->chk', u[ids], xc).astype(jnp.float32))
        return jnp.einsum('chk,chkd->cd', (g * act).astype(xc.dtype), v[ids])

    out = lax.map(chunk, tok.reshape(-1, PEER_CHUNK, shp[-1]))
    return out.reshape(-1, shp[-1])[:n].reshape(shp)


def memory_kv(mem, norm_mem, w_mem_kv):
    B, M, _ = mem.shape
    ks, vs = [], []
    for l in range(DEPTH):
        kv = rms_norm(mem, norm_mem[l]) @ w_mem_kv[l]
        ks.append(kv[..., :MEM_W].reshape(B, M, MEM_HEADS, MEM_HD))
        vs.append(kv[..., MEM_W:].reshape(B, M, MEM_HEADS, MEM_HD))
    return jnp.stack(ks), jnp.stack(vs)


def shared_latent(s, pos, kv_norm, w_kv_a, kv_latent_norm):
    kv = rms_norm(s, kv_norm) @ w_kv_a
    c = rms_norm(kv[..., :KV_LORA], kv_latent_norm)
    kr = rope(kv[..., KV_LORA:], pos)
    return c, kr


def run_group(x, pos, mem_k, mem_v, conv_state, rec_state, past_c, past_kr, p):
    B, T, _ = x.shape
    new_conv, new_h = [], []
    c_kv, k_rope = None, None
    for l in range(DEPTH):
        hn = rms_norm(x, p['norm_mix'][l])
        if l < N_A_LAYERS:
            proj = hn @ p['a_w_in'][l]
            xb, gb, qm = proj[..., :D_RNN], proj[..., D_RNN:2 * D_RNN], proj[..., 2 * D_RNN:]
            y_r, cs, hs = rg_lru_branch(xb, conv_state[l], rec_state[l], p['a_conv_w'][l], p['a_conv_b'][l],
                                        p['a_gate_r_w'][l], p['a_gate_r_b'][l], p['a_gate_i_w'][l],
                                        p['a_gate_i_b'][l], p['a_lambda'][l])
            new_conv.append(cs)
            new_h.append(hs)
            mixed = jnp.concatenate([y_r * jax.nn.gelu(gb), mem_attention(qm, mem_k[l], mem_v[l])], axis=-1)
            x = x + mixed @ p['a_w_out'][l]
        else:
            j = l - N_A_LAYERS
            if c_kv is None:
                c_kv, k_rope = shared_latent(x, pos, p['kv_norm'], p['w_kv_a'], p['kv_latent_norm'])
            proj = hn @ p['b_w_in'][j]
            q = rms_norm(proj[..., :Q_LORA], p['b_q_norm'][j]) @ p['b_w_uq'][j]
            q = q.reshape(B, T, MLA_HEADS, QK_NOPE + QK_ROPE)
            q_nope, q_rope = q[..., :QK_NOPE], rope(q[..., QK_NOPE:], pos)
            if past_c is None:
                y_a = mla_prompt_attention(q_nope, q_rope, c_kv, k_rope, p['w_uk'], p['w_uv'])
            else:
                y_a = mla_sample_attention(q_nope, q_rope, past_c, past_kr, c_kv, k_rope, p['w_uk'], p['w_uv'])
            mixed = jnp.concatenate([y_a, mem_attention(proj[..., Q_LORA:], mem_k[l], mem_v[l])], axis=-1)
            x = x + mixed @ p['b_w_out'][j]
        x = x + peer_ffn(rms_norm(x, p['norm_ffn'][l]), p['peer_wq'][l], p['peer_sub_keys'][l],
                         p['peer_u'][l], p['peer_v'][l])
    y = rms_norm(x, p['final_norm'])
    return y, c_kv, k_rope, jnp.stack(new_conv), jnp.stack(new_h)


def setup_inputs(seed: int = 0) -> dict:
    key = jax.random.key(seed)
    ks = iter(jax.random.split(key, 48))
    f32 = jnp.float32

    def nrm(shape, scale=1.0):
        return jax.random.normal(next(ks), shape, f32) * scale

    def gain(shape):
        return 1.0 + 0.02 * nrm(shape)

    n_pages = PAST_LEN // PAGE_SIZE
    n_used = DEC_BATCH * n_pages
    n_pool = n_used + n_used // 4
    x_prompt = nrm((BATCH, SEQ, D_MODEL))
    x_sample = nrm((DEC_BATCH, DEC_SEQ, D_MODEL))
    cache_latent = nrm((n_pool, PAGE_SIZE, KV_LORA))
    cache_k_rope = nrm((n_pool, PAGE_SIZE, QK_ROPE))
    state_conv = nrm((N_A_LAYERS, DEC_BATCH, CONV_W - 1, D_RNN))
    state_rglru = nrm((N_A_LAYERS, DEC_BATCH, D_RNN), 0.5)
    cache_mem_k = nrm((DEPTH, DEC_BATCH, N_MEM, MEM_HEADS, MEM_HD))
    cache_mem_v = nrm((DEPTH, DEC_BATCH, N_MEM, MEM_HEADS, MEM_HD))
    page_table = jax.random.permutation(next(ks), n_pool)[:n_used].reshape(DEC_BATCH, n_pages).astype(jnp.int32)
    mem_prompt = nrm((BATCH, N_MEM, D_MODEL))
    lam_u = jax.random.uniform(next(ks), (N_A_LAYERS, D_RNN), f32, minval=0.9, maxval=0.999)
    a_lambda = jnp.log(lam_u) - jnp.log1p(-lam_u)
    return {
        'x_prompt': x_prompt,
        'x_sample': x_sample,
        'cache_latent': cache_latent,
        'cache_k_rope': cache_k_rope,
        'state_conv': state_conv,
        'state_rglru': state_rglru,
        'cache_mem_k': cache_mem_k,
        'cache_mem_v': cache_mem_v,
        'page_table': page_table,
        'mem_prompt': mem_prompt,
        'norm_mix': gain((DEPTH, D_MODEL)),
        'norm_ffn': gain((DEPTH, D_MODEL)),
        'norm_mem': gain((DEPTH, D_MODEL)),
        'w_mem_kv': nrm((DEPTH, D_MODEL, 2 * MEM_W), D_MODEL ** -0.5),
        'a_w_in': nrm((N_A_LAYERS, D_MODEL, 2 * D_RNN + MEM_W), D_MODEL ** -0.5),
        'a_conv_w': nrm((N_A_LAYERS, CONV_W, D_RNN), CONV_W ** -0.5),
        'a_conv_b': nrm((N_A_LAYERS, D_RNN), 0.01),
        'a_gate_r_w': nrm((N_A_LAYERS, RNN_BLOCKS, RNN_BLOCK_W, RNN_BLOCK_W), RNN_BLOCK_W ** -0.5),
        'a_gate_r_b': nrm((N_A_LAYERS, D_RNN), 0.01),
        'a_gate_i_w': nrm((N_A_LAYERS, RNN_BLOCKS, RNN_BLOCK_W, RNN_BLOCK_W), RNN_BLOCK_W ** -0.5),
        'a_gate_i_b': nrm((N_A_LAYERS, D_RNN), 0.01),
        'a_lambda': a_lambda,
        'a_w_out': nrm((N_A_LAYERS, D_RNN + MEM_W, D_MODEL), (D_RNN + MEM_W) ** -0.5),
        'kv_norm': gain((D_MODEL,)),
        'w_kv_a': nrm((D_MODEL, KV_LORA + QK_ROPE), D_MODEL ** -0.5),
        'kv_latent_norm': gain((KV_LORA,)),
        'w_uk': nrm((KV_LORA, MLA_HEADS, QK_NOPE), KV_LORA ** -0.5),
        'w_uv': nrm((KV_LORA, MLA_HEADS, V_DIM), KV_LORA ** -0.5),
        'b_w_in': nrm((N_B_LAYERS, D_MODEL, Q_LORA + MEM_W), D_MODEL ** -0.5),
        'b_q_norm': gain((N_B_LAYERS, Q_LORA)),
        'b_w_uq': nrm((N_B_LAYERS, Q_LORA, MLA_HEADS * (QK_NOPE + QK_ROPE)), Q_LORA ** -0.5),
        'b_w_out': nrm((N_B_LAYERS, MLA_HEADS * V_DIM + MEM_W, D_MODEL), (MLA_HEADS * V_DIM + MEM_W) ** -0.5),
        'peer_wq': nrm((DEPTH, D_MODEL, PEER_HEADS * PEER_QK), D_MODEL ** -0.5),
        'peer_sub_keys': nrm((DEPTH, 2, N_KEYS, PEER_HALF), PEER_HALF ** -0.5),
        'peer_u': nrm((DEPTH, N_EXPERTS, D_MODEL), D_MODEL ** -0.5),
        'peer_v': nrm((DEPTH, N_EXPERTS, D_MODEL), 0.25),
        'final_norm': gain((D_MODEL,)),
    }


def reference(x_prompt, x_sample, cache_latent, cache_k_rope, state_conv, state_rglru, cache_mem_k,
              cache_mem_v, page_table, mem_prompt, norm_mix, norm_ffn, norm_mem, w_mem_kv, a_w_in,
              a_conv_w, a_conv_b, a_gate_r_w, a_gate_r_b, a_gate_i_w, a_gate_i_b, a_lambda, a_w_out,
              kv_norm, w_kv_a, kv_latent_norm, w_uk, w_uv, b_w_in, b_q_norm, b_w_uq, b_w_out,
              peer_wq, peer_sub_keys, peer_u, peer_v, final_norm):
    p = dict(norm_mix=norm_mix, norm_ffn=norm_ffn, a_w_in=a_w_in, a_conv_w=a_conv_w, a_conv_b=a_conv_b,
             a_gate_r_w=a_gate_r_w, a_gate_r_b=a_gate_r_b, a_gate_i_w=a_gate_i_w, a_gate_i_b=a_gate_i_b,
             a_lambda=a_lambda, a_w_out=a_w_out, kv_norm=kv_norm, w_kv_a=w_kv_a,
             kv_latent_norm=kv_latent_norm, w_uk=w_uk, w_uv=w_uv, b_w_in=b_w_in, b_q_norm=b_q_norm,
             b_w_uq=b_w_uq, b_w_out=b_w_out, peer_wq=peer_wq, peer_sub_keys=peer_sub_keys,
             peer_u=peer_u, peer_v=peer_v, final_norm=final_norm)

    B, T, _ = x_prompt.shape
    pos_p = jnp.arange(T)
    mem_k_prompt, mem_v_prompt = memory_kv(mem_prompt, norm_mem, w_mem_kv)
    conv0 = jnp.zeros((N_A_LAYERS, B, CONV_W - 1, D_RNN), x_prompt.dtype)
    h0 = jnp.zeros((N_A_LAYERS, B, D_RNN), x_prompt.dtype)
    y_prompt, latent_prompt, k_rope_prompt, conv_prompt, rglru_prompt = run_group(
        x_prompt, pos_p, mem_k_prompt, mem_v_prompt, conv0, h0, None, None, p)

    DB, TS, _ = x_sample.shape
    past_len = page_table.shape[1] * PAGE_SIZE
    c_past = cache_latent[page_table].reshape(DB, past_len, KV_LORA)
    kr_past = cache_k_rope[page_table].reshape(DB, past_len, QK_ROPE)
    pos_s = past_len + jnp.arange(TS)
    y_sample, latent_sample, k_rope_sample, conv_sample, rglru_sample = run_group(
        x_sample, pos_s, cache_mem_k, cache_mem_v, state_conv, state_rglru, c_past, kr_past, p)

    return (y_prompt, y_sample, latent_prompt, k_rope_prompt, conv_prompt, rglru_prompt,
            mem_k_prompt, mem_v_prompt, latent_sample, k_rope_sample, conv_sample, rglru_sample)
```

```python
import functools

import jax
import jax.numpy as jnp
from jax import lax
from jax.experimental import pallas as pl
from jax.experimental.pallas import tpu as pltpu

F32 = jnp.float32
BF16 = jnp.bfloat16

NORM_EPS = 1e-6
RG_C = 8.0
RNN_BLOCKS = 8
CONV_W = 4
MLA_HEADS = 16
QK_NOPE = 64
QK_ROPE = 32
V_DIM = 64
KV_LORA = 256
Q_LORA = 384
ROPE_BASE = 10000.0
MLA_SCALE = (QK_NOPE + QK_ROPE) ** -0.5
MEM_HEADS = 4
MEM_HD = 128
MEM_W = MEM_HEADS * MEM_HD
PEER_HEADS = 8
N_KEYS = 128
PEER_HALF = 64
PEER_TOPK = 16

LANES = 128
SUBLANES = 8
HEAD_PAD = LANES
NEW_ROWS = 16
PAGED_CHUNK_PAGES = 16
VMEM_LIMIT = 56 * 1024 * 1024


def _cparams(*sem):
    return pltpu.CompilerParams(dimension_semantics=sem, vmem_limit_bytes=VMEM_LIMIT)


def _nt_dot(a, b):
    return lax.dot_general(a, b, (((1,), (1,)), ((), ())), preferred_element_type=F32)


def _dot(a, b):
    return jnp.dot(a, b, preferred_element_type=F32)


def _rms(x, g):
    return x * lax.rsqrt(jnp.mean(x * x, axis=-1, keepdims=True) + NORM_EPS) * g


def _split_bf16(x):
    hi = x.astype(BF16)
    lo = (x - hi.astype(F32)).astype(BF16)
    return hi, lo


def _const_spec(shape):
    nd = len(shape)
    return pl.BlockSpec(shape, lambda *_: (0,) * nd)


def _norm_matmul_kernel(x_ref, g_ref, *refs, n_out):
    xn = _rms(x_ref[...], g_ref[...]).astype(BF16)
    for w_ref, o_ref in zip(refs[:n_out], refs[n_out:]):
        o_ref[...] = _dot(xn, w_ref[...]).astype(o_ref.dtype)


def norm_matmul(x, g, ws, out_dtypes, tm):
    m, k = x.shape
    tm = min(tm, m)
    return pl.pallas_call(
        functools.partial(_norm_matmul_kernel, n_out=len(ws)),
        grid=(m // tm,),
        in_specs=[pl.BlockSpec((tm, k), lambda i: (i, 0)), _const_spec((1, k))]
        + [_const_spec(w.shape) for w in ws],
        out_specs=[pl.BlockSpec((tm, w.shape[1]), lambda i: (i, 0)) for w in ws],
        out_shape=[jax.ShapeDtypeStruct((m, w.shape[1]), dt) for w, dt in zip(ws, out_dtypes)],
        compiler_params=_cparams("parallel"),
        name="norm_matmul",
    )(x, g.reshape(1, k), *ws)


def _rglru_gates(conv, wri_ref, br, bi, lam):
    bw = conv.shape[1] // RNN_BLOCKS
    conv_bf = conv.astype(BF16)
    rs, gs = [], []
    for n in range(RNN_BLOCKS):
        ri = _dot(conv_bf[:, n * bw:(n + 1) * bw], wri_ref[n])
        rs.append(ri[:, :bw])
        gs.append(ri[:, bw:])
    r = jax.nn.sigmoid(jnp.concatenate(rs, axis=-1) + br)
    i = jax.nn.sigmoid(jnp.concatenate(gs, axis=-1) + bi)
    z = -lam
    softplus = jnp.maximum(z, 0.0) + jnp.log1p(jnp.exp(-jnp.abs(z)))
    log_a = -RG_C * r * softplus
    a = jnp.exp(log_a)
    b = jnp.sqrt(-jnp.tanh(log_a) * (a * a + 1.0)) * (i * conv)
    return a, b


def _rglru_seq_kernel(xb_ref, gb_ref, cs_ref, h0_ref, cw_ref, cb_ref, wri_ref, br_ref, bi_ref, lam_ref,
                      o_ref, hl_ref, ext_s, a_s, b_s, h_s, *, tt):
    t = pl.program_id(1)
    pad = SUBLANES - (CONV_W - 1)

    @pl.when(t == 0)
    def _():
        ext_s[pad:SUBLANES, :] = cs_ref[0]
        h_s[...] = h0_ref[0]

    xb = xb_ref[0]
    ext_s[SUBLANES:SUBLANES + tt, :] = xb
    conv = cb_ref[...] + xb * cw_ref[CONV_W - 1:CONV_W, :]
    for k in range(CONV_W - 1):
        conv = conv + ext_s[pad + k:pad + k + tt, :] * cw_ref[k:k + 1, :]
    ext_s[pad:SUBLANES, :] = ext_s[tt + pad:tt + SUBLANES, :]

    a, b = _rglru_gates(conv, wri_ref, br_ref[...], bi_ref[...], lam_ref[...])
    a_s[...] = a
    b_s[...] = b

    sub = lax.broadcasted_iota(jnp.int32, (SUBLANES, a.shape[1]), 0)

    def rows(g, h):
        grp = pl.ds(pl.multiple_of(g * SUBLANES, SUBLANES), SUBLANES)
        a8, b8 = a_s[grp, :], b_s[grp, :]
        h8 = jnp.zeros_like(a8)
        for j in range(SUBLANES):
            h = a8[j:j + 1] * h + b8[j:j + 1]
            h8 = jnp.where(sub == j, h, h8)
        b_s[grp, :] = h8
        return h

    h = lax.fori_loop(0, tt // SUBLANES, rows, h_s[...])
    h_s[...] = h
    hl_ref[0] = h
    o_ref[0] = (b_s[...] * jax.nn.gelu(gb_ref[0])).astype(o_ref.dtype)


def rglru_seq(xb, gb, conv_state, h0, cw, cb, wri, br, bi, lam, tt):
    bsz, t, c = xb.shape
    tt = min(tt, t)
    row = lambda v: v.reshape(1, c)
    tile = pl.BlockSpec((1, tt, c), lambda b, i: (b, i, 0))
    return pl.pallas_call(
        functools.partial(_rglru_seq_kernel, tt=tt),
        grid=(bsz, t // tt),
        in_specs=[tile, tile,
                  pl.BlockSpec((1, CONV_W - 1, c), lambda b, i: (b, 0, 0)),
                  pl.BlockSpec((1, 1, c), lambda b, i: (b, 0, 0)),
                  _const_spec((CONV_W, c)), _const_spec((1, c)), _const_spec(wri.shape),
                  _const_spec((1, c)), _const_spec((1, c)), _const_spec((1, c))],
        out_specs=[tile, pl.BlockSpec((1, 1, c), lambda b, i: (b, 0, 0))],
        out_shape=[jax.ShapeDtypeStruct((bsz, t, c), BF16), jax.ShapeDtypeStruct((bsz, 1, c), F32)],
        scratch_shapes=[pltpu.VMEM((tt + SUBLANES, c), F32), pltpu.VMEM((tt, c), F32),
                        pltpu.VMEM((tt, c), F32), pltpu.VMEM((1, c), F32)],
        compiler_params=_cparams("parallel", "arbitrary"),
        name="rglru_seq",
    )(xb, gb, conv_state, h0.reshape(bsz, 1, c), cw, row(cb), wri, row(br), row(bi), row(lam))


def _rglru_step_kernel(ext_ref, gb_ref, h0_ref, cw_ref, cb_ref, wri_ref, br_ref, bi_ref, lam_ref,
                       o_ref, hl_ref, *, steps):
    h = h0_ref[...]
    for t in range(steps):
        conv = cb_ref[...]
        for k in range(CONV_W):
            conv = conv + ext_ref[t + k] * cw_ref[k:k + 1, :]
        a, b = _rglru_gates(conv, wri_ref, br_ref[...], bi_ref[...], lam_ref[...])
        h = a * h + b
        o_ref[t] = (h * jax.nn.gelu(gb_ref[t])).astype(o_ref.dtype)
    hl_ref[...] = h


def rglru_step(ext, gb, h0, cw, cb, wri, br, bi, lam):
    steps, bsz, c = gb.shape
    row = lambda v: v.reshape(1, c)
    return pl.pallas_call(
        functools.partial(_rglru_step_kernel, steps=steps),
        out_shape=[jax.ShapeDtypeStruct((steps, bsz, c), BF16), jax.ShapeDtypeStruct((bsz, c), F32)],
        compiler_params=pltpu.CompilerParams(vmem_limit_bytes=VMEM_LIMIT),
        name="rglru_step",
    )(ext, gb, h0, cw, row(cb), wri, row(br), row(bi), row(lam))


def _mem_attn_kernel(q_ref, k_ref, v_ref, o_ref):
    q = q_ref[0]
    outs = []
    for h in range(MEM_HEADS):
        cols = slice(h * MEM_HD, (h + 1) * MEM_HD)
        s = _nt_dot(q[:, cols], k_ref[0, :, cols].astype(BF16)) * (MEM_HD ** -0.5)
        p = jnp.exp(s - jnp.max(s, axis=-1, keepdims=True))
        l = jnp.sum(p, axis=-1, keepdims=True)
        outs.append(_dot(p.astype(BF16), v_ref[0, :, cols].astype(BF16)) / l)
    o_ref[0] = jnp.concatenate(outs, axis=-1).astype(o_ref.dtype)


def mem_attention(q, mk, mv, tt):
    g, t_in, w = q.shape
    n_mem = mk.shape[1]
    t = -(-t_in // NEW_ROWS) * NEW_ROWS
    q = jnp.pad(q, ((0, 0), (0, t - t_in), (0, 0)))
    tt = min(tt, t)
    kv_spec = pl.BlockSpec((1, n_mem, w), lambda b, i: (b, 0, 0))
    tile = pl.BlockSpec((1, tt, w), lambda b, i: (b, i, 0))
    return pl.pallas_call(
        _mem_attn_kernel,
        grid=(g, t // tt),
        in_specs=[tile, kv_spec, kv_spec],
        out_specs=tile,
        out_shape=jax.ShapeDtypeStruct((g, t, w), BF16),
        compiler_params=_cparams("parallel", "parallel"),
        name="mem_attention",
    )(q, mk, mv)[:, :t_in]


def _out_proj_kernel(x_ref, a_ref, b_ref, wa_ref, wb_ref, o_ref):
    o_ref[...] = x_ref[...] + (_dot(a_ref[...], wa_ref[...]) + _dot(b_ref[...], wb_ref[...]))


def out_proj(x, a, b, wa, wb, tm):
    m, d = x.shape
    tm = min(tm, m)
    rows = lambda n: pl.BlockSpec((tm, n), lambda i: (i, 0))
    return pl.pallas_call(
        _out_proj_kernel,
        grid=(m // tm,),
        in_specs=[rows(d), rows(a.shape[1]), rows(b.shape[1]), _const_spec(wa.shape), _const_spec(wb.shape)],
        out_specs=rows(d),
        out_shape=jax.ShapeDtypeStruct((m, d), F32),
        compiler_params=_cparams("parallel"),
        name="out_proj",
    )(x, a, b, wa, wb)


def _extract_top(vals, weights, n_top):
    out_v, out_n = [], []
    for it in range(n_top):
        m = functools.reduce(jnp.maximum, vals)
        eq = [v == m for v in vals]
        n = functools.reduce(jnp.add, [jnp.where(e, w, 0.0) for e, w in zip(eq, weights)])
        out_v.append(m)
        out_n.append(n)
        if it + 1 < n_top:
            vals = [jnp.where(e, -jnp.inf, v) for e, v in zip(eq, vals)]
    return out_v, out_n


def _peer_route_kernel(x_ref, g_ref, wq_hi_ref, wq_lo_ref, sk_hi_ref, sk_lo_ref,
                       xn_ref, s1_ref, s2_ref, r_ref, q_s, top_s, cnt_s, *, tm):
    xn = _rms(x_ref[...], g_ref[...])
    xn_ref[...] = xn.astype(BF16)
    x_hi, x_lo = _split_bf16(xn)
    q_s[...] = (_nt_dot(wq_hi_ref[...], x_hi) + _nt_dot(wq_hi_ref[...], x_lo)) + _nt_dot(wq_lo_ref[...], x_hi)

    for h in range(PEER_HEADS):
        for p, s_ref in enumerate((s1_ref, s2_ref)):
            base = (h * 2 + p) * PEER_HALF
            q_hi, q_lo = _split_bf16(q_s[base:base + PEER_HALF, :])
            s = (_dot(sk_hi_ref[p], q_hi) + _dot(sk_hi_ref[p], q_lo)) + _dot(sk_lo_ref[p], q_hi)
            s_ref[h] = s
            for it in range(PEER_TOPK):
                m = jnp.max(s, axis=0, keepdims=True)
                eq = s == m
                top_s[p, it, h:h + 1, :] = m
                cnt_s[p, it, h:h + 1, :] = jnp.sum(jnp.where(eq, 1.0, 0.0), axis=0, keepdims=True)
                if it + 1 < PEER_TOPK:
                    s = jnp.where(eq, -jnp.inf, s)

    for c in range(tm // LANES):
        lanes = slice(c * LANES, (c + 1) * LANES)
        a = [top_s[0, i, :, lanes] for i in range(PEER_TOPK)]
        b = [top_s[1, i, :, lanes] for i in range(PEER_TOPK)]
        na = [cnt_s[0, i, :, lanes] for i in range(PEER_TOPK)]
        nb = [cnt_s[1, i, :, lanes] for i in range(PEER_TOPK)]
        pairs = [(i, j) for i in range(PEER_TOPK) for j in range(PEER_TOPK) if (i + 1) * (j + 1) <= PEER_TOPK]
        vals = [a[i] + b[j] for i, j in pairs]
        mult = [na[i] * nb[j] for i, j in pairs]
        v, n = _extract_top(vals, mult, PEER_TOPK)
        thr = v[0]
        z = jnp.zeros_like(v[0])
        seen = jnp.zeros_like(v[0])
        for vk, nk in zip(v, n):
            live = seen < PEER_TOPK
            thr = jnp.where(live, vk, thr)
            z = z + jnp.where(live, jnp.minimum(nk, PEER_TOPK - seen) * jnp.exp(vk - v[0]), 0.0)
            seen = seen + nk
        h8 = PEER_HEADS
        r_ref[0 * h8:1 * h8, lanes] = thr
        r_ref[1 * h8:2 * h8, lanes] = a[0]
        r_ref[2 * h8:3 * h8, lanes] = b[0]
        r_ref[3 * h8:4 * h8, lanes] = 1.0 / z


def peer_route(x, g, wq_hi, wq_lo, sk_hi, sk_lo, tm):
    m, d = x.shape
    tm = min(tm, m)
    qw = wq_hi.shape[0]
    s_spec = pl.BlockSpec((PEER_HEADS, N_KEYS, tm), lambda i: (0, 0, i))
    s_shape = jax.ShapeDtypeStruct((PEER_HEADS, N_KEYS, m), F32)
    return pl.pallas_call(
        functools.partial(_peer_route_kernel, tm=tm),
        grid=(m // tm,),
        in_specs=[pl.BlockSpec((tm, d), lambda i: (i, 0)), _const_spec((1, d)),
                  _const_spec(wq_hi.shape), _const_spec(wq_lo.shape),
                  _const_spec(sk_hi.shape), _const_spec(sk_lo.shape)],
        out_specs=[pl.BlockSpec((tm, d), lambda i: (i, 0)), s_spec, s_spec,
                   pl.BlockSpec((4 * PEER_HEADS, tm), lambda i: (0, i))],
        out_shape=[jax.ShapeDtypeStruct((m, d), BF16), s_shape, s_shape,
                   jax.ShapeDtypeStruct((4 * PEER_HEADS, m), F32)],
        scratch_shapes=[pltpu.VMEM((qw, tm), F32),
                        pltpu.VMEM((2, PEER_TOPK, PEER_HEADS, tm), F32),
                        pltpu.VMEM((2, PEER_TOPK, PEER_HEADS, tm), F32)],
        compiler_params=_cparams("parallel"),
        name="peer_route",
    )(x, g.reshape(1, d), wq_hi, wq_lo, sk_hi, sk_lo)


def _peer_dense_kernel(xn_ref, x_ref, s1_ref, s2_ref, r_ref, u_ref, vt_ref, gf_ref, o_ref,
                       ea_s, eb_s, thr_s, acc_s, h_s, g_s, *, tm, final_norm):
    j = pl.program_id(1)
    h8 = PEER_HEADS

    @pl.when(j == 0)
    def _():
        for h in range(h8):
            ea_s[h] = jnp.exp(s1_ref[h] - r_ref[h8 + h:h8 + h + 1, :]) * r_ref[3 * h8 + h:3 * h8 + h + 1, :]
            eb_s[h] = jnp.exp(s2_ref[h] - r_ref[2 * h8 + h:2 * h8 + h + 1, :])
            thr_s[h] = jnp.broadcast_to(r_ref[h:h + 1, :], (SUBLANES, tm))
        acc_s[...] = jnp.zeros_like(acc_s)

    h_s[...] = _nt_dot(u_ref[...], xn_ref[...])

    grp = pl.ds(pl.multiple_of(j * SUBLANES, SUBLANES), SUBLANES)
    for c in range(SUBLANES):
        rows = slice(c * N_KEYS, (c + 1) * N_KEYS)
        for tc in range(tm // LANES):
            lanes = slice(tc * LANES, (tc + 1) * LANES)

            def head(h, w, c=c, lanes=lanes):
                pair = s2_ref[h, :, lanes] + s1_ref[h, grp, lanes][c:c + 1]
                gate = eb_s[h, :, lanes] * ea_s[h, grp, lanes][c:c + 1]
                return w + jnp.where(pair >= thr_s[h, 0:1, lanes], gate, 0.0)

            w = lax.fori_loop(0, h8, head, jnp.zeros((N_KEYS, LANES), F32))
            g_s[rows, lanes] = (w * jax.nn.gelu(h_s[rows, lanes])).astype(BF16)
    acc_s[...] += _dot(vt_ref[...], g_s[...])

    @pl.when(j == pl.num_programs(1) - 1)
    def _():
        y = x_ref[...] + acc_s[...].T
        if final_norm:
            y = _rms(y, gf_ref[...])
        o_ref[...] = y


def peer_dense(xn, x, s1, s2, r, u, vt, g_final, tm, final_norm):
    m, d = x.shape
    tm = min(tm, m)
    n_exp = u.shape[0]
    te = SUBLANES * N_KEYS
    s_spec = pl.BlockSpec((PEER_HEADS, N_KEYS, tm), lambda i, j: (0, 0, i))
    rows = pl.BlockSpec((tm, d), lambda i, j: (i, 0))
    return pl.pallas_call(
        functools.partial(_peer_dense_kernel, tm=tm, final_norm=final_norm),
        grid=(m // tm, n_exp // te),
        in_specs=[rows, rows, s_spec, s_spec,
                  pl.BlockSpec((4 * PEER_HEADS, tm), lambda i, j: (0, i)),
                  pl.BlockSpec((te, d), lambda i, j: (j, 0)),
                  pl.BlockSpec((d, te), lambda i, j: (0, j)),
                  _const_spec((1, d))],
        out_specs=rows,
        out_shape=jax.ShapeDtypeStruct((m, d), F32),
        scratch_shapes=[pltpu.VMEM((PEER_HEADS, N_KEYS, tm), F32), pltpu.VMEM((PEER_HEADS, N_KEYS, tm), F32),
                        pltpu.VMEM((PEER_HEADS, SUBLANES, tm), F32),
                        pltpu.VMEM((d, tm), F32), pltpu.VMEM((te, tm), F32), pltpu.VMEM((te, tm), BF16)],
        compiler_params=_cparams("parallel", "arbitrary"),
        name="peer_dense",
    )(xn, x, s1, s2, r, u, vt, g_final.reshape(1, d))


def peer_ffn(x, g, wq_hi, wq_lo, sk_hi, sk_lo, u, vt, g_final, final_norm):
    xn, s1, s2, r = peer_route(x, g, wq_hi, wq_lo, sk_hi, sk_lo, tm=256)
    return peer_dense(xn, x, s1, s2, r, u, vt, g_final, tm=512, final_norm=final_norm)


def _mla_proj_kernel(x_ref, gmix_ref, gkv_ref, wq_ref, wm_ref, wc_ref, wr_ref, wrs_ref, glat_ref, gq_ref,
                     wuq_ref, wuqs_ref, cos_ref, sin_ref, wuk_ref, wuv_ref,
                     qm_ref, c_ref, kr_ref, q_ref, *kv_refs, with_kv):
    x = x_ref[...]
    xhat = x * lax.rsqrt(jnp.mean(x * x, axis=-1, keepdims=True) + NORM_EPS)
    xn = (xhat * gmix_ref[...]).astype(BF16)
    xk = (xhat * gkv_ref[...]).astype(BF16)
    cos, sin = cos_ref[...], sin_ref[...]

    qm_ref[...] = _dot(xn, wm_ref[...]).astype(qm_ref.dtype)
    c = _rms(_dot(xk, wc_ref[...]), glat_ref[...])
    c_ref[...] = c
    kr = _dot(xk, wr_ref[...]) * cos + _dot(xk, wrs_ref[...]) * sin
    kr_ref[...] = kr

    qn = _rms(_dot(xn, wq_ref[...]), gq_ref[...]).astype(BF16)
    qa = _dot(qn, wuq_ref[...])
    qb = _dot(qn, wuqs_ref[...])
    c_bf = c.astype(BF16)
    if with_kv:
        k_ref, v_ref = kv_refs
        kn = _dot(c_bf, wuk_ref[...])
        v_ref[...] = _dot(c_bf, wuv_ref[...]).astype(v_ref.dtype)
    for h in range(MLA_HEADS):
        cols = slice(h * HEAD_PAD, (h + 1) * HEAD_PAD)
        q_ref[:, cols] = (qa[:, cols] * cos + qb[:, cols] * sin).astype(q_ref.dtype)
        if with_kv:
            k_ref[:, cols] = (kn[:, cols] + kr).astype(k_ref.dtype)


def mla_proj(x, cos_t, sin_t, w, tm, with_kv):
    m, d = x.shape
    tm = min(tm, m)
    rows = lambda n: pl.BlockSpec((tm, n), lambda i: (i, 0))
    consts = [w["g_mix"], w["g_kv"], w["w_q"], w["w_m"], w["w_c"], w["w_r"], w["w_rs"], w["g_lat"], w["g_q"],
              w["w_uq"], w["w_uqs"]]
    tail = [w["w_uk"], w["w_uv"]]
    hp = MLA_HEADS * HEAD_PAD
    out_specs = [rows(MEM_W), rows(KV_LORA), rows(HEAD_PAD), rows(hp)]
    out_shape = [jax.ShapeDtypeStruct((m, MEM_W), BF16), jax.ShapeDtypeStruct((m, KV_LORA), F32),
                 jax.ShapeDtypeStruct((m, HEAD_PAD), F32), jax.ShapeDtypeStruct((m, hp), BF16)]
    if with_kv:
        out_specs += [rows(hp), rows(MLA_HEADS * V_DIM)]
        out_shape += [jax.ShapeDtypeStruct((m, hp), BF16), jax.ShapeDtypeStruct((m, MLA_HEADS * V_DIM), BF16)]
    return pl.pallas_call(
        functools.partial(_mla_proj_kernel, with_kv=with_kv),
        grid=(m // tm,),
        in_specs=[rows(d)] + [_const_spec(a.shape) for a in consts] + [rows(HEAD_PAD), rows(HEAD_PAD)]
        + [_const_spec(a.shape) for a in tail],
        out_specs=out_specs,
        out_shape=out_shape,
        compiler_params=_cparams("parallel"),
        name="mla_proj",
    )(x, *consts, cos_t, sin_t, *tail)


def _flash_kernel(q_ref, k_ref, v_ref, o_ref, *, bq, bk):
    qi = pl.program_id(2)
    row = lax.broadcasted_iota(jnp.int32, (bq, bk), 0)
    col = lax.broadcasted_iota(jnp.int32, (bq, bk), 1)
    outs = []
    for hh in range(2):
        cols = slice(hh * HEAD_PAD, (hh + 1) * HEAD_PAD)
        q = q_ref[0, :, cols]

        def block(j, carry, masked):
            m, l, acc = carry
            ks = pl.ds(pl.multiple_of(j * bk, bk), bk)
            s = _nt_dot(q, k_ref[0, ks, cols]) * MLA_SCALE
            if masked:
                s = jnp.where(col <= row, s, -jnp.inf)
            m_new = jnp.maximum(m, jnp.max(s, axis=-1, keepdims=True))
            alpha = jnp.exp(m - m_new)
            p = jnp.exp(s - m_new)
            l = alpha * l + jnp.sum(p, axis=-1, keepdims=True)
            acc = alpha * acc + _dot(p.astype(BF16), v_ref[0, ks, :])
            return m_new, l, acc

        init = (jnp.full((bq, 1), -jnp.inf, F32), jnp.zeros((bq, 1), F32), jnp.zeros((bq, 2 * V_DIM), F32))
        carry = lax.fori_loop(0, qi, functools.partial(block, masked=False), init)
        m, l, acc = block(qi, carry, masked=True)
        outs.append(acc / l)
    lane = lax.broadcasted_iota(jnp.int32, (bq, 2 * V_DIM), 1)
    o_ref[0] = jnp.where(lane < V_DIM, outs[0], outs[1]).astype(o_ref.dtype)


def flash_mla(q, k, v, bq):
    b, t, _ = q.shape
    bq = min(bq, t)
    return pl.pallas_call(
        functools.partial(_flash_kernel, bq=bq, bk=bq),
        grid=(b, MLA_HEADS // 2, t // bq),
        in_specs=[pl.BlockSpec((1, bq, 2 * HEAD_PAD), lambda b_, h, i: (b_, i, h)),
                  pl.BlockSpec((1, t, 2 * HEAD_PAD), lambda b_, h, i: (b_, 0, h)),
                  pl.BlockSpec((1, t, 2 * V_DIM), lambda b_, h, i: (b_, 0, h))],
        out_specs=pl.BlockSpec((1, bq, 2 * V_DIM), lambda b_, h, i: (b_, i, h)),
        out_shape=jax.ShapeDtypeStruct((b, t, MLA_HEADS * V_DIM), BF16),
        compiler_params=_cparams("parallel", "parallel", "arbitrary"),
        name="flash_mla",
    )(q, k, v)


def _bmm_kernel(a_ref, b_ref, o_ref):
    o_ref[0] = _dot(a_ref[0], b_ref[0]).astype(o_ref.dtype)


def head_matmul(a, b, out_dtype):
    h, m, k = a.shape
    n = b.shape[2]
    return pl.pallas_call(
        _bmm_kernel,
        grid=(h,),
        in_specs=[pl.BlockSpec((1, m, k), lambda i: (i, 0, 0)), pl.BlockSpec((1, k, n), lambda i: (i, 0, 0))],
        out_specs=pl.BlockSpec((1, m, n), lambda i: (i, 0, 0)),
        out_shape=jax.ShapeDtypeStruct((h, m, n), out_dtype),
        compiler_params=_cparams("parallel"),
        name="head_matmul",
    )(a, b)


def _paged_kernel(pt_ref, ql_ref, qr_ref, cn_ref, kn_ref, lat_hbm, rope_hbm, o_ref,
                  cbuf, rbuf, sem, *, n_chunks, chunk_pages, page, steps):
    b = pl.program_id(0)
    nb = pl.num_programs(0)

    def copies(bb, c, slot):
        out = []
        for p in range(chunk_pages):
            pid = pt_ref[bb, c * chunk_pages + p]
            dst = pl.ds(p * page, page)
            out.append(pltpu.make_async_copy(lat_hbm.at[pid], cbuf.at[slot, dst], sem.at[slot, 0]))
            out.append(pltpu.make_async_copy(rope_hbm.at[pid], rbuf.at[slot, dst], sem.at[slot, 1]))
        return out

    def start(bb, c, slot):
        for cp in copies(bb, c, slot):
            cp.start()

    @pl.when(b == 0)
    def _():
        start(0, 0, 0)

    ql = ql_ref[0]
    qr = qr_ref[0]
    n_rows = ql.shape[0]

    def merge(carry, s, values):
        m, l, acc = carry
        m_new = jnp.maximum(m, jnp.max(s, axis=-1, keepdims=True))
        alpha = jnp.exp(m - m_new)
        p = jnp.exp(s - m_new)
        l = alpha * l + jnp.sum(p, axis=-1, keepdims=True)
        acc = alpha * acc + _dot(p.astype(BF16), values)
        return m_new, l, acc

    def chunk(c, carry):
        g = b * n_chunks + c
        slot = lax.rem(g, 2)
        last = c == n_chunks - 1

        @pl.when(jnp.logical_not(last))
        def _():
            start(b, c + 1, 1 - slot)

        @pl.when(jnp.logical_and(last, b + 1 < nb))
        def _():
            start(b + 1, 0, 1 - slot)

        for cp in copies(b, c, slot):
            cp.wait()
        lat = cbuf[slot].astype(BF16)
        kr = rbuf[slot].astype(BF16)
        s = (_nt_dot(ql, lat) + _nt_dot(qr, kr)) * MLA_SCALE
        return merge(carry, s, lat)

    init = (jnp.full((n_rows, 1), -jnp.inf, F32), jnp.zeros((n_rows, 1), F32), jnp.zeros((n_rows, KV_LORA), F32))
    carry = lax.fori_loop(0, n_chunks, chunk, init)

    cn = cn_ref[0].astype(BF16)
    s = (_nt_dot(ql, cn) + _nt_dot(qr, kn_ref[0].astype(BF16))) * MLA_SCALE
    t_row = lax.broadcasted_iota(jnp.int32, s.shape, 0) // MLA_HEADS
    t_col = lax.broadcasted_iota(jnp.int32, s.shape, 1)
    s = jnp.where(jnp.logical_and(t_col <= t_row, t_col < steps), s, -jnp.inf)
    m, l, acc = merge(carry, s, cn)
    o_ref[0] = acc / l


def paged_mla(page_table, ql, qr, c_new, kr_new, cache_latent, cache_k_rope, steps, chunk_pages):
    bsz, n_rows, _ = ql.shape
    n_pages = page_table.shape[1]
    page = cache_latent.shape[1]
    chunk_pages = min(chunk_pages, n_pages)
    n_chunks = n_pages // chunk_pages
    per_b = lambda shape: pl.BlockSpec((1,) + shape, lambda b, pt: (b, 0, 0))
    grid_spec = pltpu.PrefetchScalarGridSpec(
        num_scalar_prefetch=1,
        grid=(bsz,),
        in_specs=[per_b((n_rows, KV_LORA)), per_b((n_rows, QK_ROPE)),
                  per_b((NEW_ROWS, KV_LORA)), per_b((NEW_ROWS, QK_ROPE)),
                  pl.BlockSpec(memory_space=pl.ANY), pl.BlockSpec(memory_space=pl.ANY)],
        out_specs=per_b((n_rows, KV_LORA)),
        scratch_shapes=[pltpu.VMEM((2, chunk_pages * page, KV_LORA), F32),
                        pltpu.VMEM((2, chunk_pages * page, QK_ROPE), F32),
                        pltpu.SemaphoreType.DMA((2, 2))],
    )
    return pl.pallas_call(
        functools.partial(_paged_kernel, n_chunks=n_chunks, chunk_pages=chunk_pages, page=page, steps=steps),
        grid_spec=grid_spec,
        out_shape=jax.ShapeDtypeStruct((bsz, n_rows, KV_LORA), F32),
        compiler_params=_cparams("arbitrary"),
        name="paged_mla",
    )(page_table, ql, qr, c_new, kr_new, cache_latent, cache_k_rope)


def _pad_heads(w, width):
    k, h, _ = w.shape
    return jnp.pad(w, ((0, 0), (0, 0), (0, HEAD_PAD - width))).reshape(k, h * HEAD_PAD)


def _rope_swap(w):
    half = QK_ROPE // 2
    return jnp.concatenate([w[..., half:], w[..., :half]], axis=-1)


def _rope_tables(pos):
    half = QK_ROPE // 2
    freqs = ROPE_BASE ** (-jnp.arange(half, dtype=F32) / half)
    ang = pos.astype(F32)[:, None] * freqs
    cos, sin = jnp.cos(ang), jnp.sin(ang)
    n = pos.shape[0]
    tail = jnp.zeros((n, HEAD_PAD - QK_NOPE - QK_ROPE), F32)
    cos_t = jnp.concatenate([jnp.ones((n, QK_NOPE), F32), cos, cos, tail], axis=-1)
    sin_t = jnp.concatenate([jnp.zeros((n, QK_NOPE), F32), -sin, sin, tail], axis=-1)
    return cos_t, sin_t


def _prep_weights(p):
    bf = lambda a: a.astype(BF16)
    d = p["a_w_in"].shape[1]
    w = {}
    a_in = p["a_w_in"][0]
    d_rnn = p["a_conv_w"].shape[2]
    w["a_in"] = [bf(a_in[:, :d_rnn]), bf(a_in[:, d_rnn:2 * d_rnn]), bf(a_in[:, 2 * d_rnn:])]
    w["a_wri"] = bf(jnp.concatenate([p["a_gate_r_w"][0], p["a_gate_i_w"][0]], axis=-1))
    w["a_out"] = (bf(p["a_w_out"][0][:d_rnn]), bf(p["a_w_out"][0][d_rnn:]))
    b_in = p["b_w_in"][0]
    kv_a = p["w_kv_a"]
    zeros_n = jnp.zeros((d, QK_NOPE), F32)
    zeros_t = jnp.zeros((d, HEAD_PAD - QK_NOPE - QK_ROPE), F32)
    w_r = kv_a[:, KV_LORA:]
    uq = p["b_w_uq"][0].reshape(Q_LORA, MLA_HEADS, QK_NOPE + QK_ROPE)
    uq_n, uq_r = uq[..., :QK_NOPE], uq[..., QK_NOPE:]
    w["mla"] = {
        "g_mix": p["norm_mix"][1].reshape(1, d), "g_kv": p["kv_norm"].reshape(1, d),
        "w_q": bf(b_in[:, :Q_LORA]), "w_m": bf(b_in[:, Q_LORA:]),
        "w_c": bf(kv_a[:, :KV_LORA]),
        "w_r": bf(jnp.concatenate([zeros_n, w_r, zeros_t], axis=-1)),
        "w_rs": bf(jnp.concatenate([zeros_n, _rope_swap(w_r), zeros_t], axis=-1)),
        "g_lat": p["kv_latent_norm"].reshape(1, KV_LORA), "g_q": p["b_q_norm"][0].reshape(1, Q_LORA),
        "w_uq": bf(_pad_heads(jnp.concatenate([uq_n, uq_r], axis=-1), QK_NOPE + QK_ROPE)),
        "w_uqs": bf(_pad_heads(jnp.concatenate([jnp.zeros_like(uq_n), _rope_swap(uq_r)], axis=-1),
                               QK_NOPE + QK_ROPE)),
        "w_uk": bf(_pad_heads(p["w_uk"], QK_NOPE)),
        "w_uv": bf(p["w_uv"].reshape(KV_LORA, MLA_HEADS * V_DIM)),
    }
    w["uk_t"] = bf(jnp.transpose(p["w_uk"], (1, 2, 0)))
    w["uv_h"] = bf(jnp.transpose(p["w_uv"], (1, 0, 2)))
    hv = MLA_HEADS * V_DIM
    w["b_out"] = (bf(p["b_w_out"][0][:hv]), bf(p["b_w_out"][0][hv:]))
    w["peer"] = []
    for l in range(p["peer_wq"].shape[0]):
        wq_hi, wq_lo = _split_bf16(p["peer_wq"][l].T)
        sk_hi, sk_lo = _split_bf16(p["peer_sub_keys"][l])
        w["peer"].append((wq_hi, wq_lo, sk_hi, sk_lo, bf(p["peer_u"][l]), bf(p["peer_v"][l].T)))
    return w


def _run_group(x, pos, mem_k, mem_v, conv_state, h0, past, p, w):
    bsz, t, d = x.shape
    m = bsz * t
    xf = x.reshape(m, d)
    decode = past is not None
    tm = 512

    xb, gb, qm = norm_matmul(xf, p["norm_mix"][0], w["a_in"], (F32, F32, BF16), tm)
    c = xb.shape[1]
    xb3, gb3 = xb.reshape(bsz, t, c), gb.reshape(bsz, t, c)
    rg = (p["a_conv_w"][0], p["a_conv_b"][0], w["a_wri"], p["a_gate_r_b"][0], p["a_gate_i_b"][0], p["a_lambda"][0])
    if decode:
        ext = jnp.concatenate([jnp.swapaxes(conv_state, 0, 1), jnp.swapaxes(xb3, 0, 1)], axis=0)
        gated, h_last = rglru_step(ext, jnp.swapaxes(gb3, 0, 1), h0, *rg)
        gated = jnp.swapaxes(gated, 0, 1).reshape(m, c)
        new_conv = jnp.swapaxes(ext[t:], 0, 1)
    else:
        gated, h_last = rglru_seq(xb3, gb3, conv_state, h0, *rg, tt=512)
        gated = gated.reshape(m, c)
        h_last = h_last.reshape(bsz, c)
        new_conv = jnp.concatenate([conv_state, xb3], axis=1)[:, t:] if t < CONV_W - 1 else xb3[:, t - (CONV_W - 1):]
    n_mem = mem_k.shape[2]
    mk = mem_k.reshape(mem_k.shape[0], bsz, n_mem, MEM_W)
    mv = mem_v.reshape(mem_v.shape[0], bsz, n_mem, MEM_W)
    att = mem_attention(qm.reshape(bsz, t, MEM_W), mk[0], mv[0], tt=512).reshape(m, MEM_W)
    xf = out_proj(xf, gated, att, *w["a_out"], tm)
    xf = peer_ffn(xf, p["norm_ffn"][0], *w["peer"][0], p["final_norm"], final_norm=False)

    cos_t, sin_t = _rope_tables(jnp.tile(pos, bsz))
    outs = mla_proj(xf, cos_t, sin_t, w["mla"], tm, with_kv=not decode)
    qm, c_kv, kr_pad, q_pad = outs[:4]
    k_rope = kr_pad[:, QK_NOPE:QK_NOPE + QK_ROPE]
    hv = MLA_HEADS * V_DIM
    if decode:
        cache_latent, cache_k_rope, page_table = past
        q4 = q_pad.reshape(m, MLA_HEADS, HEAD_PAD)
        q_nope = jnp.swapaxes(q4[..., :QK_NOPE], 0, 1)
        q_lat = head_matmul(q_nope, w["uk_t"], BF16)
        n_rows = t * MLA_HEADS
        q_lat = jnp.swapaxes(q_lat, 0, 1).reshape(bsz, n_rows, KV_LORA)
        q_rope = q4[..., QK_NOPE:QK_NOPE + QK_ROPE].reshape(bsz, n_rows, QK_ROPE)
        pad_rows = lambda a: jnp.pad(a.reshape(bsz, t, -1), ((0, 0), (0, NEW_ROWS - t), (0, 0)))
        o_lat = paged_mla(page_table, q_lat, q_rope, pad_rows(c_kv), pad_rows(k_rope),
                          cache_latent, cache_k_rope, steps=t, chunk_pages=PAGED_CHUNK_PAGES)
        o_lat = jnp.swapaxes(o_lat.reshape(m, MLA_HEADS, KV_LORA), 0, 1).astype(BF16)
        y_a = jnp.swapaxes(head_matmul(o_lat, w["uv_h"], BF16), 0, 1).reshape(m, hv)
    else:
        k_pad, v = outs[4:]
        hp = MLA_HEADS * HEAD_PAD
        y_a = flash_mla(q_pad.reshape(bsz, t, hp), k_pad.reshape(bsz, t, hp), v.reshape(bsz, t, hv), bq=512)
        y_a = y_a.reshape(m, hv)
    att = mem_attention(qm.reshape(bsz, t, MEM_W), mk[1], mv[1], tt=512).reshape(m, MEM_W)
    xf = out_proj(xf, y_a, att, *w["b_out"], tm)
    y = peer_ffn(xf, p["norm_ffn"][1], *w["peer"][1], p["final_norm"], final_norm=True)

    return (y.reshape(bsz, t, d), c_kv.reshape(bsz, t, KV_LORA), k_rope.reshape(bsz, t, QK_ROPE),
            new_conv[None], h_last[None])


def kernel(x_prompt, x_sample, cache_latent, cache_k_rope, state_conv, state_rglru, cache_mem_k, cache_mem_v,
           page_table, mem_prompt, norm_mix, norm_ffn, norm_mem, w_mem_kv, a_w_in, a_conv_w, a_conv_b,
           a_gate_r_w, a_gate_r_b, a_gate_i_w, a_gate_i_b, a_lambda, a_w_out, kv_norm, w_kv_a, kv_latent_norm,
           w_uk, w_uv, b_w_in, b_q_norm, b_w_uq, b_w_out, peer_wq, peer_sub_keys, peer_u, peer_v, final_norm):
    p = dict(norm_mix=norm_mix, norm_ffn=norm_ffn, a_w_in=a_w_in, a_conv_w=a_conv_w, a_conv_b=a_conv_b,
             a_gate_r_w=a_gate_r_w, a_gate_r_b=a_gate_r_b, a_gate_i_w=a_gate_i_w, a_gate_i_b=a_gate_i_b,
             a_lambda=a_lambda, a_w_out=a_w_out, kv_norm=kv_norm, w_kv_a=w_kv_a, kv_latent_norm=kv_latent_norm,
             w_uk=w_uk, w_uv=w_uv, b_w_in=b_w_in, b_q_norm=b_q_norm, b_w_uq=b_w_uq, b_w_out=b_w_out,
             peer_wq=peer_wq, peer_sub_keys=peer_sub_keys, peer_u=peer_u, peer_v=peer_v, final_norm=final_norm)
    w = _prep_weights(p)
    depth = norm_mem.shape[0]

    b, t, d = x_prompt.shape
    n_mem = mem_prompt.shape[1]
    mem_flat = mem_prompt.reshape(b * n_mem, d)
    kvs = [norm_matmul(mem_flat, norm_mem[l], [w_mem_kv[l].astype(BF16)], (F32,), 512)[0] for l in range(depth)]
    mem_k_prompt = jnp.stack([kv[:, :MEM_W].reshape(b, n_mem, MEM_HEADS, MEM_HD) for kv in kvs])
    mem_v_prompt = jnp.stack([kv[:, MEM_W:].reshape(b, n_mem, MEM_HEADS, MEM_HD) for kv in kvs])
    d_rnn = a_conv_w.shape[2]
    conv0 = jnp.zeros((b, CONV_W - 1, d_rnn), x_prompt.dtype)
    h0 = jnp.zeros((b, d_rnn), x_prompt.dtype)
    y_prompt, latent_prompt, k_rope_prompt, conv_prompt, rglru_prompt = _run_group(
        x_prompt, jnp.arange(t), mem_k_prompt, mem_v_prompt, conv0, h0, None, p, w)

    ts = x_sample.shape[1]
    past_len = page_table.shape[1] * cache_latent.shape[1]
    y_sample, latent_sample, k_rope_sample, conv_sample, rglru_sample = _run_group(
        x_sample, past_len + jnp.arange(ts), cache_mem_k, cache_mem_v, state_conv[0], state_rglru[0],
        (cache_latent, cache_k_rope, page_table), p, w)

    return (y_prompt, y_sample, latent_prompt, k_rope_prompt, conv_prompt, rglru_prompt,
            mem_k_prompt, mem_v_prompt, latent_sample, k_rope_sample, conv_sample, rglru_sample)
```

```python
import functools
import math

import jax
import jax.numpy as jnp
from jax import lax
from jax.experimental import pallas as pl
from jax.experimental.pallas import tpu as pltpu

F32 = jnp.float32
BF16 = jnp.bfloat16

NORM_EPS = 1e-6
RG_C = 8.0
RNN_BLOCKS = 8
CONV_W = 4
MLA_HEADS = 16
QK_NOPE = 64
QK_ROPE = 32
V_DIM = 64
KV_LORA = 256
Q_LORA = 384
ROPE_BASE = 10000.0
MLA_SCALE = (QK_NOPE + QK_ROPE) ** -0.5
SCALE_LOG2E = MLA_SCALE * math.log2(math.e)
MEM_HEADS = 4
MEM_HD = 128
MEM_W = MEM_HEADS * MEM_HD
PEER_HEADS = 8
N_KEYS = 128
PEER_HALF = 64
PEER_TOPK = 16

LANES = 128
SUBLANES = 8
HEAD_PAD = LANES
NEW_ROWS = 16
PAGED_CHUNK_PAGES = 16
VMEM_LIMIT = 56 * 1024 * 1024


def _cparams(*sem):
    return pltpu.CompilerParams(dimension_semantics=sem, vmem_limit_bytes=VMEM_LIMIT)


def _nt_dot(a, b):
    return lax.dot_general(a, b, (((1,), (1,)), ((), ())), preferred_element_type=F32)


def _dot(a, b):
    return jnp.dot(a, b, preferred_element_type=F32)


def _rms(x, g):
    return x * lax.rsqrt(jnp.mean(x * x, axis=-1, keepdims=True) + NORM_EPS) * g


def _split_bf16(x):
    hi = x.astype(BF16)
    lo = (x - hi.astype(F32)).astype(BF16)
    return hi, lo


def _const_spec(shape):
    nd = len(shape)
    return pl.BlockSpec(shape, lambda *_: (0,) * nd)


def _norm_matmul_kernel(x_ref, g_ref, *refs, n_out):
    xn = _rms(x_ref[...], g_ref[...]).astype(BF16)
    for w_ref, o_ref in zip(refs[:n_out], refs[n_out:]):
        o_ref[...] = _dot(xn, w_ref[...]).astype(o_ref.dtype)


def norm_matmul(x, g, ws, out_dtypes, tm):
    m, k = x.shape
    tm = min(tm, m)
    return pl.pallas_call(
        functools.partial(_norm_matmul_kernel, n_out=len(ws)),
        grid=(m // tm,),
        in_specs=[pl.BlockSpec((tm, k), lambda i: (i, 0)), _const_spec((1, k))]
        + [_const_spec(w.shape) for w in ws],
        out_specs=[pl.BlockSpec((tm, w.shape[1]), lambda i: (i, 0)) for w in ws],
        out_shape=[jax.ShapeDtypeStruct((m, w.shape[1]), dt) for w, dt in zip(ws, out_dtypes)],
        compiler_params=_cparams("parallel"),
        name="norm_matmul",
    )(x, g.reshape(1, k), *ws)


def _rglru_gates(conv, wri_ref, br, bi, lam):
    bw = conv.shape[1] // RNN_BLOCKS
    conv_bf = conv.astype(BF16)
    rs, gs = [], []
    for n in range(RNN_BLOCKS):
        ri = _dot(conv_bf[:, n * bw:(n + 1) * bw], wri_ref[n])
        rs.append(ri[:, :bw])
        gs.append(ri[:, bw:])
    r = jax.nn.sigmoid(jnp.concatenate(rs, axis=-1) + br)
    i = jax.nn.sigmoid(jnp.concatenate(gs, axis=-1) + bi)
    z = -lam
    softplus = jnp.maximum(z, 0.0) + jnp.log1p(jnp.exp(-jnp.abs(z)))
    log_a = -RG_C * r * softplus
    a = jnp.exp(log_a)
    b = jnp.sqrt(-jnp.tanh(log_a) * (a * a + 1.0)) * (i * conv)
    return a, b


def _rglru_seq_kernel(xb_ref, gb_ref, cs_ref, h0_ref, cw_ref, cb_ref, wri_ref, br_ref, bi_ref, lam_ref,
                      o_ref, hl_ref, ext_s, a_s, b_s, h_s, *, tt):
    t = pl.program_id(1)
    pad = SUBLANES - (CONV_W - 1)

    @pl.when(t == 0)
    def _():
        ext_s[pad:SUBLANES, :] = cs_ref[0]
        h_s[...] = h0_ref[0]

    xb = xb_ref[0]
    ext_s[SUBLANES:SUBLANES + tt, :] = xb
    conv = cb_ref[...] + xb * cw_ref[CONV_W - 1:CONV_W, :]
    for k in range(CONV_W - 1):
        conv = conv + ext_s[pad + k:pad + k + tt, :] * cw_ref[k:k + 1, :]
    ext_s[pad:SUBLANES, :] = ext_s[tt + pad:tt + SUBLANES, :]

    a, b = _rglru_gates(conv, wri_ref, br_ref[...], bi_ref[...], lam_ref[...])
    a_s[...] = a
    b_s[...] = b

    sub = lax.broadcasted_iota(jnp.int32, (SUBLANES, a.shape[1]), 0)

    def rows(g, h):
        grp = pl.ds(pl.multiple_of(g * SUBLANES, SUBLANES), SUBLANES)
        a8, b8 = a_s[grp, :], b_s[grp, :]
        h8 = jnp.zeros_like(a8)
        for j in range(SUBLANES):
            h = a8[j:j + 1] * h + b8[j:j + 1]
            h8 = jnp.where(sub == j, h, h8)
        b_s[grp, :] = h8
        return h

    h = lax.fori_loop(0, tt // SUBLANES, rows, h_s[...])
    h_s[...] = h
    hl_ref[0] = h
    o_ref[0] = (b_s[...] * jax.nn.gelu(gb_ref[0])).astype(o_ref.dtype)


def rglru_seq(xb, gb, conv_state, h0, cw, cb, wri, br, bi, lam, tt):
    bsz, t, c = xb.shape
    tt = min(tt, t)
    row = lambda v: v.reshape(1, c)
    tile = pl.BlockSpec((1, tt, c), lambda b, i: (b, i, 0))
    return pl.pallas_call(
        functools.partial(_rglru_seq_kernel, tt=tt),
        grid=(bsz, t // tt),
        in_specs=[tile, tile,
                  pl.BlockSpec((1, CONV_W - 1, c), lambda b, i: (b, 0, 0)),
                  pl.BlockSpec((1, 1, c), lambda b, i: (b, 0, 0)),
                  _const_spec((CONV_W, c)), _const_spec((1, c)), _const_spec(wri.shape),
                  _const_spec((1, c)), _const_spec((1, c)), _const_spec((1, c))],
        out_specs=[tile, pl.BlockSpec((1, 1, c), lambda b, i: (b, 0, 0))],
        out_shape=[jax.ShapeDtypeStruct((bsz, t, c), BF16), jax.ShapeDtypeStruct((bsz, 1, c), F32)],
        scratch_shapes=[pltpu.VMEM((tt + SUBLANES, c), F32), pltpu.VMEM((tt, c), F32),
                        pltpu.VMEM((tt, c), F32), pltpu.VMEM((1, c), F32)],
        compiler_params=_cparams("parallel", "arbitrary"),
        name="rglru_seq",
    )(xb, gb, conv_state, h0.reshape(bsz, 1, c), cw, row(cb), wri, row(br), row(bi), row(lam))


def _rglru_step_kernel(ext_ref, gb_ref, h0_ref, cw_ref, cb_ref, wri_ref, br_ref, bi_ref, lam_ref,
                       o_ref, hl_ref, *, steps):
    h = h0_ref[...]
    for t in range(steps):
        conv = cb_ref[...]
        for k in range(CONV_W):
            conv = conv + ext_ref[t + k] * cw_ref[k:k + 1, :]
        a, b = _rglru_gates(conv, wri_ref, br_ref[...], bi_ref[...], lam_ref[...])
        h = a * h + b
        o_ref[t] = (h * jax.nn.gelu(gb_ref[t])).astype(o_ref.dtype)
    hl_ref[...] = h


def rglru_step(ext, gb, h0, cw, cb, wri, br, bi, lam):
    steps, bsz, c = gb.shape
    row = lambda v: v.reshape(1, c)
    return pl.pallas_call(
        functools.partial(_rglru_step_kernel, steps=steps),
        out_shape=[jax.ShapeDtypeStruct((steps, bsz, c), BF16), jax.ShapeDtypeStruct((bsz, c), F32)],
        compiler_params=pltpu.CompilerParams(vmem_limit_bytes=VMEM_LIMIT),
        name="rglru_step",
    )(ext, gb, h0, cw, row(cb), wri, row(br), row(bi), row(lam))


def _mem_attn_kernel(q_ref, k_ref, v_ref, o_ref):
    q = q_ref[0]
    k = k_ref[0, 0].astype(BF16)
    v = v_ref[0, 0].astype(BF16)
    row_head = lax.broadcasted_iota(jnp.int32, (q.shape[0], k.shape[0]), 1) % MEM_HEADS
    outs = []
    for h in range(MEM_HEADS):
        s = _nt_dot(q[:, h * MEM_HD:(h + 1) * MEM_HD], k) * (MEM_HD ** -0.5)
        s = jnp.where(row_head == h, s, -jnp.inf)
        p = jnp.exp(s - jnp.max(s, axis=-1, keepdims=True))
        l = jnp.sum(p, axis=-1, keepdims=True)
        outs.append(_dot(p.astype(BF16), v) / l)
    o_ref[0] = jnp.concatenate(outs, axis=-1).astype(o_ref.dtype)


def mem_attention(q, mk, mv, layer, tt):
    g, t_in, w = q.shape
    n_rows = mk.shape[2]
    t = -(-t_in // NEW_ROWS) * NEW_ROWS
    q = jnp.pad(q, ((0, 0), (0, t - t_in), (0, 0)))
    tt = min(tt, t)
    kv_spec = pl.BlockSpec((1, 1, n_rows, MEM_HD), lambda b, i: (layer, b, 0, 0))
    tile = pl.BlockSpec((1, tt, w), lambda b, i: (b, i, 0))
    return pl.pallas_call(
        _mem_attn_kernel,
        grid=(g, t // tt),
        in_specs=[tile, kv_spec, kv_spec],
        out_specs=tile,
        out_shape=jax.ShapeDtypeStruct((g, t, w), BF16),
        compiler_params=_cparams("parallel", "parallel"),
        name="mem_attention",
    )(q, mk, mv)[:, :t_in]


def _out_proj_kernel(x_ref, a_ref, b_ref, wa_ref, wb_ref, o_ref):
    o_ref[...] = x_ref[...] + (_dot(a_ref[...], wa_ref[...]) + _dot(b_ref[...], wb_ref[...]))


def out_proj(x, a, b, wa, wb, tm):
    m, d = x.shape
    tm = min(tm, m)
    rows = lambda n: pl.BlockSpec((tm, n), lambda i: (i, 0))
    return pl.pallas_call(
        _out_proj_kernel,
        grid=(m // tm,),
        in_specs=[rows(d), rows(a.shape[1]), rows(b.shape[1]), _const_spec(wa.shape), _const_spec(wb.shape)],
        out_specs=rows(d),
        out_shape=jax.ShapeDtypeStruct((m, d), F32),
        compiler_params=_cparams("parallel"),
        name="out_proj",
    )(x, a, b, wa, wb)


def _extract_top(vals, weights, n_top):
    out_v, out_n = [], []
    for it in range(n_top):
        m = functools.reduce(jnp.maximum, vals)
        eq = [v == m for v in vals]
        n = functools.reduce(jnp.add, [jnp.where(e, w, 0.0) for e, w in zip(eq, weights)])
        out_v.append(m)
        out_n.append(n)
        if it + 1 < n_top:
            vals = [jnp.where(e, -jnp.inf, v) for e, v in zip(eq, vals)]
    return out_v, out_n


def _peer_route_kernel(x_ref, g_ref, wq_hi_ref, wq_lo_ref, sk_hi_ref, sk_lo_ref,
                       xn_ref, s1_ref, s2_ref, r_ref, q_s, top_s, cnt_s, *, tm):
    xn = _rms(x_ref[...], g_ref[...])
    xn_ref[...] = xn.astype(BF16)
    x_hi, x_lo = _split_bf16(xn)
    q_s[...] = (_nt_dot(wq_hi_ref[...], x_hi) + _nt_dot(wq_hi_ref[...], x_lo)) + _nt_dot(wq_lo_ref[...], x_hi)

    for h in range(PEER_HEADS):
        for p, s_ref in enumerate((s1_ref, s2_ref)):
            base = (h * 2 + p) * PEER_HALF
            q_hi, q_lo = _split_bf16(q_s[base:base + PEER_HALF, :])
            s = (_dot(sk_hi_ref[p], q_hi) + _dot(sk_hi_ref[p], q_lo)) + _dot(sk_lo_ref[p], q_hi)
            s_ref[h] = s
            for it in range(PEER_TOPK):
                m = jnp.max(s, axis=0, keepdims=True)
                eq = s == m
                top_s[p, it, h:h + 1, :] = m
                cnt_s[p, it, h:h + 1, :] = jnp.sum(jnp.where(eq, 1.0, 0.0), axis=0, keepdims=True)
                if it + 1 < PEER_TOPK:
                    s = jnp.where(eq, -jnp.inf, s)

    for c in range(tm // LANES):
        lanes = slice(c * LANES, (c + 1) * LANES)
        a = [top_s[0, i, :, lanes] for i in range(PEER_TOPK)]
        b = [top_s[1, i, :, lanes] for i in range(PEER_TOPK)]
        na = [cnt_s[0, i, :, lanes] for i in range(PEER_TOPK)]
        nb = [cnt_s[1, i, :, lanes] for i in range(PEER_TOPK)]
        pairs = [(i, j) for i in range(PEER_TOPK) for j in range(PEER_TOPK) if (i + 1) * (j + 1) <= PEER_TOPK]
        vals = [a[i] + b[j] for i, j in pairs]
        mult = [na[i] * nb[j] for i, j in pairs]
        v, n = _extract_top(vals, mult, PEER_TOPK)
        thr = v[0]
        z = jnp.zeros_like(v[0])
        seen = jnp.zeros_like(v[0])
        for vk, nk in zip(v, n):
            live = seen < PEER_TOPK
            thr = jnp.where(live, vk, thr)
            z = z + jnp.where(live, jnp.minimum(nk, PEER_TOPK - seen) * jnp.exp(vk - v[0]), 0.0)
            seen = seen + nk
        h8 = PEER_HEADS
        r_ref[0 * h8:1 * h8, lanes] = thr
        r_ref[1 * h8:2 * h8, lanes] = a[0]
        r_ref[2 * h8:3 * h8, lanes] = b[0]
        r_ref[3 * h8:4 * h8, lanes] = 1.0 / z


def peer_route(x, g, wq_hi, wq_lo, sk_hi, sk_lo, tm):
    m, d = x.shape
    tm = min(tm, m)
    qw = wq_hi.shape[0]
    s_spec = pl.BlockSpec((PEER_HEADS, N_KEYS, tm), lambda i: (0, 0, i))
    s_shape = jax.ShapeDtypeStruct((PEER_HEADS, N_KEYS, m), F32)
    return pl.pallas_call(
        functools.partial(_peer_route_kernel, tm=tm),
        grid=(m // tm,),
        in_specs=[pl.BlockSpec((tm, d), lambda i: (i, 0)), _const_spec((1, d)),
                  _const_spec(wq_hi.shape), _const_spec(wq_lo.shape),
                  _const_spec(sk_hi.shape), _const_spec(sk_lo.shape)],
        out_specs=[pl.BlockSpec((tm, d), lambda i: (i, 0)), s_spec, s_spec,
                   pl.BlockSpec((4 * PEER_HEADS, tm), lambda i: (0, i))],
        out_shape=[jax.ShapeDtypeStruct((m, d), BF16), s_shape, s_shape,
                   jax.ShapeDtypeStruct((4 * PEER_HEADS, m), F32)],
        scratch_shapes=[pltpu.VMEM((qw, tm), F32),
                        pltpu.VMEM((2, PEER_TOPK, PEER_HEADS, tm), F32),
                        pltpu.VMEM((2, PEER_TOPK, PEER_HEADS, tm), F32)],
        compiler_params=_cparams("parallel"),
        name="peer_route",
    )(x, g.reshape(1, d), wq_hi, wq_lo, sk_hi, sk_lo)


def _peer_dense_kernel(xn_ref, x_ref, s1_ref, s2_ref, r_ref, u_ref, vt_ref, gf_ref, o_ref,
                       ea_s, eb_s, acc_s, g_s, *, tm, final_norm):
    j = pl.program_id(1)
    h8 = PEER_HEADS

    @pl.when(j == 0)
    def _():
        for h in range(h8):
            ea_s[h] = jnp.exp(s1_ref[h] - r_ref[h8 + h:h8 + h + 1, :]) * r_ref[3 * h8 + h:3 * h8 + h + 1, :]
            eb_s[h] = jnp.exp(s2_ref[h] - r_ref[2 * h8 + h:2 * h8 + h + 1, :])
        acc_s[...] = jnp.zeros_like(acc_s)

    xn = xn_ref[...]
    grp = pl.ds(pl.multiple_of(j * SUBLANES, SUBLANES), SUBLANES)
    for cp in range(SUBLANES // 2):
        prow = slice(cp * 2 * N_KEYS, (cp + 1) * 2 * N_KEYS)
        hpre = _nt_dot(u_ref[prow, :], xn)
        for cc in range(2):
            c = cp * 2 + cc
            rows = slice(c * N_KEYS, (c + 1) * N_KEYS)
            for tc in range(tm // LANES):
                lanes = slice(tc * LANES, (tc + 1) * LANES)
                w = None
                for h in range(h8):
                    pair = s2_ref[h, :, lanes] + s1_ref[h, grp, lanes][c:c + 1]
                    gate = eb_s[h, :, lanes] * ea_s[h, grp, lanes][c:c + 1]
                    term = jnp.where(pair >= r_ref[h:h + 1, lanes], gate, 0.0)
                    w = term if w is None else w + term
                act = jax.nn.gelu(hpre[cc * N_KEYS:(cc + 1) * N_KEYS, lanes])
                g_s[rows, lanes] = (w * act).astype(BF16)
        acc_s[...] += _dot(vt_ref[:, prow], g_s[prow, :])

    @pl.when(j == pl.num_programs(1) - 1)
    def _():
        y = x_ref[...] + acc_s[...].T
        if final_norm:
            y = _rms(y, gf_ref[...])
        o_ref[...] = y


def peer_dense(xn, x, s1, s2, r, u, vt, g_final, tm, final_norm):
    m, d = x.shape
    tm = min(tm, m)
    n_exp = u.shape[0]
    te = SUBLANES * N_KEYS
    s_spec = pl.BlockSpec((PEER_HEADS, N_KEYS, tm), lambda i, j: (0, 0, i))
    rows = pl.BlockSpec((tm, d), lambda i, j: (i, 0))
    return pl.pallas_call(
        functools.partial(_peer_dense_kernel, tm=tm, final_norm=final_norm),
        grid=(m // tm, n_exp // te),
        in_specs=[rows, rows, s_spec, s_spec,
                  pl.BlockSpec((4 * PEER_HEADS, tm), lambda i, j: (0, i)),
                  pl.BlockSpec((te, d), lambda i, j: (j, 0)),
                  pl.BlockSpec((d, te), lambda i, j: (0, j)),
                  _const_spec((1, d))],
        out_specs=rows,
        out_shape=jax.ShapeDtypeStruct((m, d), F32),
        scratch_shapes=[pltpu.VMEM((PEER_HEADS, N_KEYS, tm), F32), pltpu.VMEM((PEER_HEADS, N_KEYS, tm), F32),
                        pltpu.VMEM((d, tm), F32), pltpu.VMEM((te, tm), BF16)],
        compiler_params=_cparams("parallel", "arbitrary"),
        name="peer_dense",
    )(xn, x, s1, s2, r, u, vt, g_final.reshape(1, d))


def peer_ffn(x, g, wq_hi, wq_lo, sk_hi, sk_lo, u, vt, g_final, final_norm):
    xn, s1, s2, r = peer_route(x, g, wq_hi, wq_lo, sk_hi, sk_lo, tm=256)
    return peer_dense(xn, x, s1, s2, r, u, vt, g_final, tm=512, final_norm=final_norm)


def _mla_proj_kernel(x_ref, gmix_ref, gkv_ref, wq_ref, wm_ref, wc_ref, wr_ref, wrs_ref, glat_ref, gq_ref,
                     wuq_ref, wuqs_ref, cos_ref, sin_ref, wuk_ref, wuv_ref,
                     qm_ref, c_ref, kr_ref, q_ref, *kv_refs, with_kv):
    x = x_ref[...]
    xhat = x * lax.rsqrt(jnp.mean(x * x, axis=-1, keepdims=True) + NORM_EPS)
    xn = (xhat * gmix_ref[...]).astype(BF16)
    xk = (xhat * gkv_ref[...]).astype(BF16)
    cos, sin = cos_ref[...], sin_ref[...]

    qm_ref[...] = _dot(xn, wm_ref[...]).astype(qm_ref.dtype)
    c = _rms(_dot(xk, wc_ref[...]), glat_ref[...])
    c_ref[...] = c
    kr = _dot(xk, wr_ref[...]) * cos + _dot(xk, wrs_ref[...]) * sin
    kr_ref[...] = kr

    qn = _rms(_dot(xn, wq_ref[...]), gq_ref[...]).astype(BF16)
    qa = _dot(qn, wuq_ref[...])
    qb = _dot(qn, wuqs_ref[...])
    c_bf = c.astype(BF16)
    if with_kv:
        k_ref, v_ref = kv_refs
        kn = _dot(c_bf, wuk_ref[...])
        v_ref[...] = _dot(c_bf, wuv_ref[...]).astype(v_ref.dtype)
    for h in range(MLA_HEADS):
        cols = slice(h * HEAD_PAD, (h + 1) * HEAD_PAD)
        q_ref[:, cols] = (qa[:, cols] * cos + qb[:, cols] * sin).astype(q_ref.dtype)
        if with_kv:
            k_ref[:, cols] = (kn[:, cols] + kr).astype(k_ref.dtype)


def mla_proj(x, cos_t, sin_t, w, tm, with_kv):
    m, d = x.shape
    tm = min(tm, m)
    rows = lambda n: pl.BlockSpec((tm, n), lambda i: (i, 0))
    consts = [w["g_mix"], w["g_kv"], w["w_q"], w["w_m"], w["w_c"], w["w_r"], w["w_rs"], w["g_lat"], w["g_q"],
              w["w_uq"], w["w_uqs"]]
    tail = [w["w_uk"], w["w_uv"]]
    hp = MLA_HEADS * HEAD_PAD
    out_specs = [rows(MEM_W), rows(KV_LORA), rows(HEAD_PAD), rows(hp)]
    out_shape = [jax.ShapeDtypeStruct((m, MEM_W), BF16), jax.ShapeDtypeStruct((m, KV_LORA), F32),
                 jax.ShapeDtypeStruct((m, HEAD_PAD), F32), jax.ShapeDtypeStruct((m, hp), BF16)]
    if with_kv:
        out_specs += [rows(hp), rows(MLA_HEADS * V_DIM)]
        out_shape += [jax.ShapeDtypeStruct((m, hp), BF16), jax.ShapeDtypeStruct((m, MLA_HEADS * V_DIM), BF16)]
    return pl.pallas_call(
        functools.partial(_mla_proj_kernel, with_kv=with_kv),
        grid=(m // tm,),
        in_specs=[rows(d)] + [_const_spec(a.shape) for a in consts] + [rows(HEAD_PAD), rows(HEAD_PAD)]
        + [_const_spec(a.shape) for a in tail],
        out_specs=out_specs,
        out_shape=out_shape,
        compiler_params=_cparams("parallel"),
        name="mla_proj",
    )(x, *consts, cos_t, sin_t, *tail)


def _flash_kernel(q_ref, k_ref, v_ref, o_ref, *, bq, bk):
    qi = pl.program_id(2)
    row = lax.broadcasted_iota(jnp.int32, (bq, bk), 0)
    col = lax.broadcasted_iota(jnp.int32, (bq, bk), 1)
    outs = []
    for hh in range(2):
        cols = slice(hh * HEAD_PAD, (hh + 1) * HEAD_PAD)
        q = q_ref[0, :, cols]

        def block(j, carry, masked):
            m, l, acc = carry
            ks = pl.ds(pl.multiple_of(j * bk, bk), bk)
            s = _nt_dot(q, k_ref[0, ks, cols])
            if masked:
                s = jnp.where(col <= row, s, -jnp.inf)
            m_new = jnp.maximum(m, jnp.max(s, axis=-1, keepdims=True))
            alpha = jnp.exp2((m - m_new) * SCALE_LOG2E)
            p = jnp.exp2((s - m_new) * SCALE_LOG2E)
            l = alpha * l + jnp.sum(p, axis=-1, keepdims=True)
            acc = alpha * acc + _dot(p.astype(BF16), v_ref[0, ks, :])
            return m_new, l, acc

        init = (jnp.full((bq, 1), -jnp.inf, F32), jnp.zeros((bq, 1), F32), jnp.zeros((bq, 2 * V_DIM), F32))
        carry = lax.fori_loop(0, qi, functools.partial(block, masked=False), init)
        m, l, acc = block(qi, carry, masked=True)
        outs.append(acc / l)
    lane = lax.broadcasted_iota(jnp.int32, (bq, 2 * V_DIM), 1)
    o_ref[0] = jnp.where(lane < V_DIM, outs[0], outs[1]).astype(o_ref.dtype)


def flash_mla(q, k, v, bq):
    b, t, _ = q.shape
    bq = min(bq, t)
    return pl.pallas_call(
        functools.partial(_flash_kernel, bq=bq, bk=bq),
        grid=(b, MLA_HEADS // 2, t // bq),
        in_specs=[pl.BlockSpec((1, bq, 2 * HEAD_PAD), lambda b_, h, i: (b_, i, h)),
                  pl.BlockSpec((1, t, 2 * HEAD_PAD), lambda b_, h, i: (b_, 0, h)),
                  pl.BlockSpec((1, t, 2 * V_DIM), lambda b_, h, i: (b_, 0, h))],
        out_specs=pl.BlockSpec((1, bq, 2 * V_DIM), lambda b_, h, i: (b_, i, h)),
        out_shape=jax.ShapeDtypeStruct((b, t, MLA_HEADS * V_DIM), BF16),
        compiler_params=_cparams("parallel", "parallel", "arbitrary"),
        name="flash_mla",
    )(q, k, v)


def _bmm_kernel(a_ref, b_ref, o_ref):
    o_ref[0] = _dot(a_ref[0], b_ref[0]).astype(o_ref.dtype)


def head_matmul(a, b, out_dtype):
    h, m, k = a.shape
    n = b.shape[2]
    return pl.pallas_call(
        _bmm_kernel,
        grid=(h,),
        in_specs=[pl.BlockSpec((1, m, k), lambda i: (i, 0, 0)), pl.BlockSpec((1, k, n), lambda i: (i, 0, 0))],
        out_specs=pl.BlockSpec((1, m, n), lambda i: (i, 0, 0)),
        out_shape=jax.ShapeDtypeStruct((h, m, n), out_dtype),
        compiler_params=_cparams("parallel"),
        name="head_matmul",
    )(a, b)


def _paged_kernel(pt_ref, ql_ref, qr_ref, cn_ref, kn_ref, lat_hbm, rope_hbm, o_ref,
                  cbuf, rbuf, sem, *, n_chunks, chunk_pages, page, steps):
    b = pl.program_id(0)
    nb = pl.num_programs(0)

    def copies(bb, c, slot):
        out = []
        for p in range(chunk_pages):
            pid = pt_ref[bb, c * chunk_pages + p]
            dst = pl.ds(p * page, page)
            out.append(pltpu.make_async_copy(lat_hbm.at[pid], cbuf.at[slot, dst], sem.at[slot, 0]))
            out.append(pltpu.make_async_copy(rope_hbm.at[pid], rbuf.at[slot, :, dst], sem.at[slot, 1]))
        return out

    def start(bb, c, slot):
        for cp in copies(bb, c, slot):
            cp.start()

    @pl.when(b == 0)
    def _():
        start(0, 0, 0)

    ql = ql_ref[0]
    qr = qr_ref[0]
    n_rows = ql.shape[0]

    def merge(carry, s, values):
        m, l, acc = carry
        m_new = jnp.maximum(m, jnp.max(s, axis=-1, keepdims=True))
        alpha = jnp.exp2((m - m_new) * SCALE_LOG2E)
        p = jnp.exp2((s - m_new) * SCALE_LOG2E)
        l = alpha * l + jnp.sum(p, axis=-1, keepdims=True)
        acc = alpha * acc + _dot(p.astype(BF16), values)
        return m_new, l, acc

    def chunk(c, carry):
        g = b * n_chunks + c
        slot = lax.rem(g, 2)
        last = c == n_chunks - 1

        @pl.when(jnp.logical_not(last))
        def _():
            start(b, c + 1, 1 - slot)

        @pl.when(jnp.logical_and(last, b + 1 < nb))
        def _():
            start(b + 1, 0, 1 - slot)

        for cp in copies(b, c, slot):
            cp.wait()
        lat = cbuf[slot].astype(BF16)
        kr_t = rbuf[slot].astype(BF16)
        return merge(carry, _nt_dot(ql, lat) + _dot(qr, kr_t), lat)

    init = (jnp.full((n_rows, 1), -jnp.inf, F32), jnp.zeros((n_rows, 1), F32), jnp.zeros((n_rows, KV_LORA), F32))
    carry = lax.fori_loop(0, n_chunks, chunk, init)

    cn = cn_ref[0].astype(BF16)
    s = _nt_dot(ql, cn) + _nt_dot(qr, kn_ref[0].astype(BF16))
    t_row = lax.broadcasted_iota(jnp.int32, s.shape, 0) // MLA_HEADS
    t_col = lax.broadcasted_iota(jnp.int32, s.shape, 1)
    s = jnp.where(jnp.logical_and(t_col <= t_row, t_col < steps), s, -jnp.inf)
    m, l, acc = merge(carry, s, cn)
    o_ref[0] = acc / l


def paged_mla(page_table, ql, qr, c_new, kr_new, cache_latent, cache_k_rope_t, steps, chunk_pages):
    bsz, n_rows, _ = ql.shape
    n_pages = page_table.shape[1]
    page = cache_latent.shape[1]
    chunk_pages = min(chunk_pages, n_pages)
    n_chunks = n_pages // chunk_pages
    per_b = lambda shape: pl.BlockSpec((1,) + shape, lambda b, pt: (b, 0, 0))
    grid_spec = pltpu.PrefetchScalarGridSpec(
        num_scalar_prefetch=1,
        grid=(bsz,),
        in_specs=[per_b((n_rows, KV_LORA)), per_b((n_rows, QK_ROPE)),
                  per_b((NEW_ROWS, KV_LORA)), per_b((NEW_ROWS, QK_ROPE)),
                  pl.BlockSpec(memory_space=pl.ANY), pl.BlockSpec(memory_space=pl.ANY)],
        out_specs=per_b((n_rows, KV_LORA)),
        scratch_shapes=[pltpu.VMEM((2, chunk_pages * page, KV_LORA), F32),
                        pltpu.VMEM((2, QK_ROPE, chunk_pages * page), F32),
                        pltpu.SemaphoreType.DMA((2, 2))],
    )
    return pl.pallas_call(
        functools.partial(_paged_kernel, n_chunks=n_chunks, chunk_pages=chunk_pages, page=page, steps=steps),
        grid_spec=grid_spec,
        out_shape=jax.ShapeDtypeStruct((bsz, n_rows, KV_LORA), F32),
        compiler_params=_cparams("arbitrary"),
        name="paged_mla",
    )(page_table, ql, qr, c_new, kr_new, cache_latent, cache_k_rope_t)


def _pad_heads(w, width):
    k, h, _ = w.shape
    return jnp.pad(w, ((0, 0), (0, 0), (0, HEAD_PAD - width))).reshape(k, h * HEAD_PAD)


def _rope_swap(w):
    half = QK_ROPE // 2
    return jnp.concatenate([w[..., half:], w[..., :half]], axis=-1)


def _rope_tables(pos):
    half = QK_ROPE // 2
    freqs = ROPE_BASE ** (-jnp.arange(half, dtype=F32) / half)
    ang = pos.astype(F32)[:, None] * freqs
    cos, sin = jnp.cos(ang), jnp.sin(ang)
    n = pos.shape[0]
    tail = jnp.zeros((n, HEAD_PAD - QK_NOPE - QK_ROPE), F32)
    cos_t = jnp.concatenate([jnp.ones((n, QK_NOPE), F32), cos, cos, tail], axis=-1)
    sin_t = jnp.concatenate([jnp.zeros((n, QK_NOPE), F32), -sin, sin, tail], axis=-1)
    return cos_t, sin_t


def _prep_weights(p):
    bf = lambda a: a.astype(BF16)
    d = p["a_w_in"].shape[1]
    w = {}
    a_in = p["a_w_in"][0]
    d_rnn = p["a_conv_w"].shape[2]
    w["a_in"] = [bf(a_in[:, :d_rnn]), bf(a_in[:, d_rnn:2 * d_rnn]), bf(a_in[:, 2 * d_rnn:])]
    w["a_wri"] = bf(jnp.concatenate([p["a_gate_r_w"][0], p["a_gate_i_w"][0]], axis=-1))
    w["a_out"] = (bf(p["a_w_out"][0][:d_rnn]), bf(p["a_w_out"][0][d_rnn:]))
    b_in = p["b_w_in"][0]
    kv_a = p["w_kv_a"]
    zeros_n = jnp.zeros((d, QK_NOPE), F32)
    zeros_t = jnp.zeros((d, HEAD_PAD - QK_NOPE - QK_ROPE), F32)
    w_r = kv_a[:, KV_LORA:]
    uq = p["b_w_uq"][0].reshape(Q_LORA, MLA_HEADS, QK_NOPE + QK_ROPE)
    uq_n, uq_r = uq[..., :QK_NOPE], uq[..., QK_NOPE:]
    w["mla"] = {
        "g_mix": p["norm_mix"][1].reshape(1, d), "g_kv": p["kv_norm"].reshape(1, d),
        "w_q": bf(b_in[:, :Q_LORA]), "w_m": bf(b_in[:, Q_LORA:]),
        "w_c": bf(kv_a[:, :KV_LORA]),
        "w_r": bf(jnp.concatenate([zeros_n, w_r, zeros_t], axis=-1)),
        "w_rs": bf(jnp.concatenate([zeros_n, _rope_swap(w_r), zeros_t], axis=-1)),
        "g_lat": p["kv_latent_norm"].reshape(1, KV_LORA), "g_q": p["b_q_norm"][0].reshape(1, Q_LORA),
        "w_uq": bf(_pad_heads(jnp.concatenate([uq_n, uq_r], axis=-1), QK_NOPE + QK_ROPE)),
        "w_uqs": bf(_pad_heads(jnp.concatenate([jnp.zeros_like(uq_n), _rope_swap(uq_r)], axis=-1),
                               QK_NOPE + QK_ROPE)),
        "w_uk": bf(_pad_heads(p["w_uk"], QK_NOPE)),
        "w_uv": bf(p["w_uv"].reshape(KV_LORA, MLA_HEADS * V_DIM)),
    }
    w["uk_t"] = bf(jnp.transpose(p["w_uk"], (1, 2, 0)))
    w["uv_h"] = bf(jnp.transpose(p["w_uv"], (1, 0, 2)))
    hv = MLA_HEADS * V_DIM
    w["b_out"] = (bf(p["b_w_out"][0][:hv]), bf(p["b_w_out"][0][hv:]))
    w["peer"] = []
    for l in range(p["peer_wq"].shape[0]):
        wq_hi, wq_lo = _split_bf16(p["peer_wq"][l].T)
        sk_hi, sk_lo = _split_bf16(p["peer_sub_keys"][l])
        w["peer"].append((wq_hi, wq_lo, sk_hi, sk_lo, bf(p["peer_u"][l]), bf(p["peer_v"][l].T)))
    return w


def _run_group(x, pos, mem_k, mem_v, conv_state, h0, past, p, w):
    bsz, t, d = x.shape
    m = bsz * t
    xf = x.reshape(m, d)
    decode = past is not None
    tm = 512

    xb, gb, qm = norm_matmul(xf, p["norm_mix"][0], w["a_in"], (F32, F32, BF16), tm)
    c = xb.shape[1]
    xb3, gb3 = xb.reshape(bsz, t, c), gb.reshape(bsz, t, c)
    rg = (p["a_conv_w"][0], p["a_conv_b"][0], w["a_wri"], p["a_gate_r_b"][0], p["a_gate_i_b"][0], p["a_lambda"][0])
    if decode:
        ext = jnp.concatenate([jnp.swapaxes(conv_state, 0, 1), jnp.swapaxes(xb3, 0, 1)], axis=0)
        gated, h_last = rglru_step(ext, jnp.swapaxes(gb3, 0, 1), h0, *rg)
        gated = jnp.swapaxes(gated, 0, 1).reshape(m, c)
        new_conv = jnp.swapaxes(ext[t:], 0, 1)
    else:
        gated, h_last = rglru_seq(xb3, gb3, conv_state, h0, *rg, tt=512)
        gated = gated.reshape(m, c)
        h_last = h_last.reshape(bsz, c)
        new_conv = jnp.concatenate([conv_state, xb3], axis=1)[:, t:] if t < CONV_W - 1 else xb3[:, t - (CONV_W - 1):]
    n_mem = mem_k.shape[2]
    mk = mem_k.reshape(mem_k.shape[0], bsz, n_mem * MEM_HEADS, MEM_HD)
    mv = mem_v.reshape(mem_v.shape[0], bsz, n_mem * MEM_HEADS, MEM_HD)
    att = mem_attention(qm.reshape(bsz, t, MEM_W), mk, mv, 0, tt=512).reshape(m, MEM_W)
    xf = out_proj(xf, gated, att, *w["a_out"], tm)
    xf = peer_ffn(xf, p["norm_ffn"][0], *w["peer"][0], p["final_norm"], final_norm=False)

    cos_t, sin_t = _rope_tables(jnp.tile(pos, bsz))
    outs = mla_proj(xf, cos_t, sin_t, w["mla"], tm, with_kv=not decode)
    qm, c_kv, kr_pad, q_pad = outs[:4]
    k_rope = kr_pad[:, QK_NOPE:QK_NOPE + QK_ROPE]
    hv = MLA_HEADS * V_DIM
    if decode:
        cache_latent, cache_k_rope, page_table = past
        q4 = q_pad.reshape(m, MLA_HEADS, HEAD_PAD)
        q_nope = jnp.swapaxes(q4[..., :QK_NOPE], 0, 1)
        q_lat = head_matmul(q_nope, w["uk_t"], BF16)
        n_rows = t * MLA_HEADS
        q_lat = jnp.swapaxes(q_lat, 0, 1).reshape(bsz, n_rows, KV_LORA)
        q_rope = q4[..., QK_NOPE:QK_NOPE + QK_ROPE].reshape(bsz, n_rows, QK_ROPE)
        pad_rows = lambda a: jnp.pad(a.reshape(bsz, t, -1), ((0, 0), (0, NEW_ROWS - t), (0, 0)))
        o_lat = paged_mla(page_table, q_lat, q_rope, pad_rows(c_kv), pad_rows(k_rope),
                          cache_latent, jnp.swapaxes(cache_k_rope, 1, 2), steps=t, chunk_pages=PAGED_CHUNK_PAGES)
        o_lat = jnp.swapaxes(o_lat.reshape(m, MLA_HEADS, KV_LORA), 0, 1).astype(BF16)
        y_a = jnp.swapaxes(head_matmul(o_lat, w["uv_h"], BF16), 0, 1).reshape(m, hv)
    else:
        k_pad, v = outs[4:]
        hp = MLA_HEADS * HEAD_PAD
        y_a = flash_mla(q_pad.reshape(bsz, t, hp), k_pad.reshape(bsz, t, hp), v.reshape(bsz, t, hv), bq=512)
        y_a = y_a.reshape(m, hv)
    att = mem_attention(qm.reshape(bsz, t, MEM_W), mk, mv, 1, tt=512).reshape(m, MEM_W)
    xf = out_proj(xf, y_a, att, *w["b_out"], tm)
    y = peer_ffn(xf, p["norm_ffn"][1], *w["peer"][1], p["final_norm"], final_norm=True)

    return (y.reshape(bsz, t, d), c_kv.reshape(bsz, t, KV_LORA), k_rope.reshape(bsz, t, QK_ROPE),
            new_conv[None], h_last[None])


def kernel(x_prompt, x_sample, cache_latent, cache_k_rope, state_conv, state_rglru, cache_mem_k, cache_mem_v,
           page_table, mem_prompt, norm_mix, norm_ffn, norm_mem, w_mem_kv, a_w_in, a_conv_w, a_conv_b,
           a_gate_r_w, a_gate_r_b, a_gate_i_w, a_gate_i_b, a_lambda, a_w_out, kv_norm, w_kv_a, kv_latent_norm,
           w_uk, w_uv, b_w_in, b_q_norm, b_w_uq, b_w_out, peer_wq, peer_sub_keys, peer_u, peer_v, final_norm):
    p = dict(norm_mix=norm_mix, norm_ffn=norm_ffn, a_w_in=a_w_in, a_conv_w=a_conv_w, a_conv_b=a_conv_b,
             a_gate_r_w=a_gate_r_w, a_gate_r_b=a_gate_r_b, a_gate_i_w=a_gate_i_w, a_gate_i_b=a_gate_i_b,
             a_lambda=a_lambda, a_w_out=a_w_out, kv_norm=kv_norm, w_kv_a=w_kv_a, kv_latent_norm=kv_latent_norm,
             w_uk=w_uk, w_uv=w_uv, b_w_in=b_w_in, b_q_norm=b_q_norm, b_w_uq=b_w_uq, b_w_out=b_w_out,
             peer_wq=peer_wq, peer_sub_keys=peer_sub_keys, peer_u=peer_u, peer_v=peer_v, final_norm=final_norm)
    w = _prep_weights(p)
    depth = norm_mem.shape[0]

    b, t, d = x_prompt.shape
    n_mem = mem_prompt.shape[1]
    mem_flat = mem_prompt.reshape(b * n_mem, d)
    kvs = [norm_matmul(mem_flat, norm_mem[l], [w_mem_kv[l].astype(BF16)], (F32,), 512)[0] for l in range(depth)]
    mem_k_prompt = jnp.stack([kv[:, :MEM_W].reshape(b, n_mem, MEM_HEADS, MEM_HD) for kv in kvs])
    mem_v_prompt = jnp.stack([kv[:, MEM_W:].reshape(b, n_mem, MEM_HEADS, MEM_HD) for kv in kvs])
    d_rnn = a_conv_w.shape[2]
    conv0 = jnp.zeros((b, CONV_W - 1, d_rnn), x_prompt.dtype)
    h0 = jnp.zeros((b, d_rnn), x_prompt.dtype)
    y_prompt, latent_prompt, k_rope_prompt, conv_prompt, rglru_prompt = _run_group(
        x_prompt, jnp.arange(t), mem_k_prompt, mem_v_prompt, conv0, h0, None, p, w)

    ts = x_sample.shape[1]
    past_len = page_table.shape[1] * cache_latent.shape[1]
    y_sample, latent_sample, k_rope_sample, conv_sample, rglru_sample = _run_group(
        x_sample, past_len + jnp.arange(ts), cache_mem_k, cache_mem_v, state_conv[0], state_rglru[0],
        (cache_latent, cache_k_rope, page_table), p, w)

    return (y_prompt, y_sample, latent_prompt, k_rope_prompt, conv_prompt, rglru_prompt,
            mem_k_prompt, mem_v_prompt, latent_sample, k_rope_sample, conv_sample, rglru_sample)
```

```python
import functools
import math

import jax
import jax.numpy as jnp
from jax import lax
from jax.experimental import pallas as pl
from jax.experimental.pallas import tpu as pltpu

F32 = jnp.float32
BF16 = jnp.bfloat16

NORM_EPS = 1e-6
RG_C = 8.0
RNN_BLOCKS = 8
CONV_W = 4
MLA_HEADS = 16
QK_NOPE = 64
QK_ROPE = 32
V_DIM = 64
KV_LORA = 256
Q_LORA = 384
ROPE_BASE = 10000.0
MLA_SCALE = (QK_NOPE + QK_ROPE) ** -0.5
SCALE_LOG2E = MLA_SCALE * math.log2(math.e)
MEM_HEADS = 4
MEM_HD = 128
MEM_W = MEM_HEADS * MEM_HD
PEER_HEADS = 8
N_KEYS = 128
PEER_HALF = 64
PEER_TOPK = 16

LANES = 128
SUBLANES = 8
HEAD_PAD = LANES
NEW_ROWS = 16
PAGED_CHUNK_PAGES = 32
GATE_ROWS = 32
VMEM_LIMIT = 56 * 1024 * 1024


def _cparams(*sem):
    return pltpu.CompilerParams(dimension_semantics=sem, vmem_limit_bytes=VMEM_LIMIT)


def _nt_dot(a, b):
    return lax.dot_general(a, b, (((1,), (1,)), ((), ())), preferred_element_type=F32)


def _dot(a, b):
    return jnp.dot(a, b, preferred_element_type=F32)


def _rms(x, g):
    return x * lax.rsqrt(jnp.mean(x * x, axis=-1, keepdims=True) + NORM_EPS) * g


def _split_bf16(x):
    hi = x.astype(BF16)
    lo = (x - hi.astype(F32)).astype(BF16)
    return hi, lo


def _const_spec(shape):
    nd = len(shape)
    return pl.BlockSpec(shape, lambda *_: (0,) * nd)


def _norm_matmul_kernel(x_ref, g_ref, *refs, n_out):
    xn = _rms(x_ref[...], g_ref[...]).astype(BF16)
    for w_ref, o_ref in zip(refs[:n_out], refs[n_out:]):
        o_ref[...] = _dot(xn, w_ref[...]).astype(o_ref.dtype)


def norm_matmul(x, g, ws, out_dtypes, tm):
    m, k = x.shape
    tm = min(tm, m)
    return pl.pallas_call(
        functools.partial(_norm_matmul_kernel, n_out=len(ws)),
        grid=(m // tm,),
        in_specs=[pl.BlockSpec((tm, k), lambda i: (i, 0)), _const_spec((1, k))]
        + [_const_spec(w.shape) for w in ws],
        out_specs=[pl.BlockSpec((tm, w.shape[1]), lambda i: (i, 0)) for w in ws],
        out_shape=[jax.ShapeDtypeStruct((m, w.shape[1]), dt) for w, dt in zip(ws, out_dtypes)],
        compiler_params=_cparams("parallel"),
        name="norm_matmul",
    )(x, g.reshape(1, k), *ws)


def _rglru_gates(conv, wri_ref, br, bi, lam):
    bw = conv.shape[1] // RNN_BLOCKS
    conv_bf = conv.astype(BF16)
    rs, gs = [], []
    for n in range(RNN_BLOCKS):
        ri = _dot(conv_bf[:, n * bw:(n + 1) * bw], wri_ref[n])
        rs.append(ri[:, :bw])
        gs.append(ri[:, bw:])
    r = jax.nn.sigmoid(jnp.concatenate(rs, axis=-1) + br)
    i = jax.nn.sigmoid(jnp.concatenate(gs, axis=-1) + bi)
    z = -lam
    softplus = jnp.maximum(z, 0.0) + jnp.log1p(jnp.exp(-jnp.abs(z)))
    log_a = -RG_C * r * softplus
    a = jnp.exp(log_a)
    b = jnp.sqrt(-jnp.tanh(log_a) * (a * a + 1.0)) * (i * conv)
    return a, b


def _rglru_seq_kernel(xb_ref, gb_ref, cs_ref, h0_ref, cw_ref, cb_ref, wri_ref, br_ref, bi_ref, lam_ref,
                      o_ref, hl_ref, ext_s, a_s, b_s, h_s, *, tt):
    t = pl.program_id(1)
    pad = SUBLANES - (CONV_W - 1)

    @pl.when(t == 0)
    def _():
        ext_s[pad:SUBLANES, :] = cs_ref[0]
        h_s[...] = h0_ref[0]

    xb = xb_ref[0]
    ext_s[SUBLANES:SUBLANES + tt, :] = xb
    conv = cb_ref[...] + xb * cw_ref[CONV_W - 1:CONV_W, :]
    for k in range(CONV_W - 1):
        conv = conv + ext_s[pad + k:pad + k + tt, :] * cw_ref[k:k + 1, :]
    ext_s[pad:SUBLANES, :] = ext_s[tt + pad:tt + SUBLANES, :]

    a, b = _rglru_gates(conv, wri_ref, br_ref[...], bi_ref[...], lam_ref[...])
    a_s[...] = a
    b_s[...] = b

    sub = lax.broadcasted_iota(jnp.int32, (SUBLANES, a.shape[1]), 0)

    def rows(g, h):
        grp = pl.ds(pl.multiple_of(g * SUBLANES, SUBLANES), SUBLANES)
        a8, b8 = a_s[grp, :], b_s[grp, :]
        h8 = jnp.zeros_like(a8)
        for j in range(SUBLANES):
            h = a8[j:j + 1] * h + b8[j:j + 1]
            h8 = jnp.where(sub == j, h, h8)
        b_s[grp, :] = h8
        return h

    h = lax.fori_loop(0, tt // SUBLANES, rows, h_s[...])
    h_s[...] = h
    hl_ref[0] = h
    o_ref[0] = (b_s[...] * jax.nn.gelu(gb_ref[0])).astype(o_ref.dtype)


def rglru_seq(xb, gb, conv_state, h0, cw, cb, wri, br, bi, lam, tt):
    bsz, t, c = xb.shape
    tt = min(tt, t)
    row = lambda v: v.reshape(1, c)
    tile = pl.BlockSpec((1, tt, c), lambda b, i: (b, i, 0))
    return pl.pallas_call(
        functools.partial(_rglru_seq_kernel, tt=tt),
        grid=(bsz, t // tt),
        in_specs=[tile, tile,
                  pl.BlockSpec((1, CONV_W - 1, c), lambda b, i: (b, 0, 0)),
                  pl.BlockSpec((1, 1, c), lambda b, i: (b, 0, 0)),
                  _const_spec((CONV_W, c)), _const_spec((1, c)), _const_spec(wri.shape),
                  _const_spec((1, c)), _const_spec((1, c)), _const_spec((1, c))],
        out_specs=[tile, pl.BlockSpec((1, 1, c), lambda b, i: (b, 0, 0))],
        out_shape=[jax.ShapeDtypeStruct((bsz, t, c), BF16), jax.ShapeDtypeStruct((bsz, 1, c), F32)],
        scratch_shapes=[pltpu.VMEM((tt + SUBLANES, c), F32), pltpu.VMEM((tt, c), F32),
                        pltpu.VMEM((tt, c), F32), pltpu.VMEM((1, c), F32)],
        compiler_params=_cparams("parallel", "arbitrary"),
        name="rglru_seq",
    )(xb, gb, conv_state, h0.reshape(bsz, 1, c), cw, row(cb), wri, row(br), row(bi), row(lam))


def _rglru_step_kernel(ext_ref, gb_ref, h0_ref, cw_ref, cb_ref, wri_ref, br_ref, bi_ref, lam_ref,
                       o_ref, hl_ref, *, steps):
    h = h0_ref[...]
    for t in range(steps):
        conv = cb_ref[...]
        for k in range(CONV_W):
            conv = conv + ext_ref[t + k] * cw_ref[k:k + 1, :]
        a, b = _rglru_gates(conv, wri_ref, br_ref[...], bi_ref[...], lam_ref[...])
        h = a * h + b
        o_ref[t] = (h * jax.nn.gelu(gb_ref[t])).astype(o_ref.dtype)
    hl_ref[...] = h


def rglru_step(ext, gb, h0, cw, cb, wri, br, bi, lam):
    steps, bsz, c = gb.shape
    row = lambda v: v.reshape(1, c)
    return pl.pallas_call(
        functools.partial(_rglru_step_kernel, steps=steps),
        out_shape=[jax.ShapeDtypeStruct((steps, bsz, c), BF16), jax.ShapeDtypeStruct((bsz, c), F32)],
        compiler_params=pltpu.CompilerParams(vmem_limit_bytes=VMEM_LIMIT),
        name="rglru_step",
    )(ext, gb, h0, cw, row(cb), wri, row(br), row(bi), row(lam))


def _mem_attn_kernel(q_ref, k_ref, v_ref, o_ref):
    q = q_ref[0]
    k = k_ref[0, 0].astype(BF16)
    v = v_ref[0, 0].astype(BF16)
    row_head = lax.broadcasted_iota(jnp.int32, (q.shape[0], k.shape[0]), 1) % MEM_HEADS
    outs = []
    for h in range(MEM_HEADS):
        s = _nt_dot(q[:, h * MEM_HD:(h + 1) * MEM_HD], k) * (MEM_HD ** -0.5)
        s = jnp.where(row_head == h, s, -jnp.inf)
        p = jnp.exp(s - jnp.max(s, axis=-1, keepdims=True))
        l = jnp.sum(p, axis=-1, keepdims=True)
        outs.append(_dot(p.astype(BF16), v) / l)
    o_ref[0] = jnp.concatenate(outs, axis=-1).astype(o_ref.dtype)


def mem_attention(q, mk, mv, layer, tt):
    g, t_in, w = q.shape
    n_rows = mk.shape[2]
    t = -(-t_in // NEW_ROWS) * NEW_ROWS
    q = jnp.pad(q, ((0, 0), (0, t - t_in), (0, 0)))
    tt = min(tt, t)
    kv_spec = pl.BlockSpec((1, 1, n_rows, MEM_HD), lambda b, i: (layer, b, 0, 0))
    tile = pl.BlockSpec((1, tt, w), lambda b, i: (b, i, 0))
    return pl.pallas_call(
        _mem_attn_kernel,
        grid=(g, t // tt),
        in_specs=[tile, kv_spec, kv_spec],
        out_specs=tile,
        out_shape=jax.ShapeDtypeStruct((g, t, w), BF16),
        compiler_params=_cparams("parallel", "parallel"),
        name="mem_attention",
    )(q, mk, mv)[:, :t_in]


def _out_proj_kernel(x_ref, a_ref, b_ref, wa_ref, wb_ref, o_ref):
    o_ref[...] = x_ref[...] + (_dot(a_ref[...], wa_ref[...]) + _dot(b_ref[...], wb_ref[...]))


def out_proj(x, a, b, wa, wb, tm):
    m, d = x.shape
    tm = min(tm, m)
    rows = lambda n: pl.BlockSpec((tm, n), lambda i: (i, 0))
    return pl.pallas_call(
        _out_proj_kernel,
        grid=(m // tm,),
        in_specs=[rows(d), rows(a.shape[1]), rows(b.shape[1]), _const_spec(wa.shape), _const_spec(wb.shape)],
        out_specs=rows(d),
        out_shape=jax.ShapeDtypeStruct((m, d), F32),
        compiler_params=_cparams("parallel"),
        name="out_proj",
    )(x, a, b, wa, wb)


def _extract_top(vals, weights, n_top):
    out_v, out_n = [], []
    for it in range(n_top):
        m = functools.reduce(jnp.maximum, vals)
        eq = [v == m for v in vals]
        n = functools.reduce(jnp.add, [jnp.where(e, w, 0.0) for e, w in zip(eq, weights)])
        out_v.append(m)
        out_n.append(n)
        if it + 1 < n_top:
            vals = [jnp.where(e, -jnp.inf, v) for e, v in zip(eq, vals)]
    return out_v, out_n


def _peer_route_kernel(x_ref, g_ref, wq_hi_ref, wq_lo_ref, sk_hi_ref, sk_lo_ref,
                       xn_ref, s1_ref, s2_ref, r_ref, q_s, top_s, cnt_s, *, tm):
    xn = _rms(x_ref[...], g_ref[...])
    xn_ref[...] = xn.astype(BF16)
    x_hi, x_lo = _split_bf16(xn)
    q_s[...] = (_nt_dot(wq_hi_ref[...], x_hi) + _nt_dot(wq_hi_ref[...], x_lo)) + _nt_dot(wq_lo_ref[...], x_hi)

    for h in range(PEER_HEADS):
        for p, s_ref in enumerate((s1_ref, s2_ref)):
            base = (h * 2 + p) * PEER_HALF
            q_hi, q_lo = _split_bf16(q_s[base:base + PEER_HALF, :])
            s = (_dot(sk_hi_ref[p], q_hi) + _dot(sk_hi_ref[p], q_lo)) + _dot(sk_lo_ref[p], q_hi)
            for tc in range(tm // LANES):
                s_ref[h, tc] = s[:, tc * LANES:(tc + 1) * LANES]
            for it in range(PEER_TOPK):
                m = jnp.max(s, axis=0, keepdims=True)
                eq = s == m
                top_s[p, it, h:h + 1, :] = m
                cnt_s[p, it, h:h + 1, :] = jnp.sum(jnp.where(eq, 1.0, 0.0), axis=0, keepdims=True)
                if it + 1 < PEER_TOPK:
                    s = jnp.where(eq, -jnp.inf, s)

    for c in range(tm // LANES):
        lanes = slice(c * LANES, (c + 1) * LANES)
        a = [top_s[0, i, :, lanes] for i in range(PEER_TOPK)]
        b = [top_s[1, i, :, lanes] for i in range(PEER_TOPK)]
        na = [cnt_s[0, i, :, lanes] for i in range(PEER_TOPK)]
        nb = [cnt_s[1, i, :, lanes] for i in range(PEER_TOPK)]
        pairs = [(i, j) for i in range(PEER_TOPK) for j in range(PEER_TOPK) if (i + 1) * (j + 1) <= PEER_TOPK]
        vals = [a[i] + b[j] for i, j in pairs]
        mult = [na[i] * nb[j] for i, j in pairs]
        v, n = _extract_top(vals, mult, PEER_TOPK)
        thr = v[0]
        z = jnp.zeros_like(v[0])
        seen = jnp.zeros_like(v[0])
        for vk, nk in zip(v, n):
            live = seen < PEER_TOPK
            thr = jnp.where(live, vk, thr)
            z = z + jnp.where(live, jnp.minimum(nk, PEER_TOPK - seen) * jnp.exp(vk - v[0]), 0.0)
            seen = seen + nk
        h8 = PEER_HEADS
        r_ref[0 * h8:1 * h8, lanes] = thr
        r_ref[1 * h8:2 * h8, lanes] = a[0]
        r_ref[2 * h8:3 * h8, lanes] = b[0]
        r_ref[3 * h8:4 * h8, lanes] = 1.0 / z


def peer_route(x, g, wq_hi, wq_lo, sk_hi, sk_lo, tm):
    m, d = x.shape
    tm = min(tm, m)
    qw = wq_hi.shape[0]
    s_spec = pl.BlockSpec((PEER_HEADS, tm // LANES, N_KEYS, LANES), lambda i: (0, i, 0, 0))
    s_shape = jax.ShapeDtypeStruct((PEER_HEADS, m // LANES, N_KEYS, LANES), F32)
    return pl.pallas_call(
        functools.partial(_peer_route_kernel, tm=tm),
        grid=(m // tm,),
        in_specs=[pl.BlockSpec((tm, d), lambda i: (i, 0)), _const_spec((1, d)),
                  _const_spec(wq_hi.shape), _const_spec(wq_lo.shape),
                  _const_spec(sk_hi.shape), _const_spec(sk_lo.shape)],
        out_specs=[pl.BlockSpec((tm, d), lambda i: (i, 0)), s_spec, s_spec,
                   pl.BlockSpec((4 * PEER_HEADS, tm), lambda i: (0, i))],
        out_shape=[jax.ShapeDtypeStruct((m, d), BF16), s_shape, s_shape,
                   jax.ShapeDtypeStruct((4 * PEER_HEADS, m), F32)],
        scratch_shapes=[pltpu.VMEM((qw, tm), F32),
                        pltpu.VMEM((2, PEER_TOPK, PEER_HEADS, tm), F32),
                        pltpu.VMEM((2, PEER_TOPK, PEER_HEADS, tm), F32)],
        compiler_params=_cparams("parallel"),
        name="peer_route",
    )(x, g.reshape(1, d), wq_hi, wq_lo, sk_hi, sk_lo)


def _peer_dense_kernel(xn_ref, x_ref, s1_ref, s2_ref, r_ref, u_ref, vt_ref, gf_ref, o_ref,
                       ea_s, eb_s, acc_s, g_s, *, tm, final_norm):
    j = pl.program_id(1)
    h8 = PEER_HEADS

    @pl.when(j == 0)
    def _():
        for h in range(h8):
            for tc in range(tm // LANES):
                lanes = slice(tc * LANES, (tc + 1) * LANES)
                inv_z = r_ref[3 * h8 + h:3 * h8 + h + 1, lanes]
                ea_s[h, tc] = jnp.exp(s1_ref[h, tc] - r_ref[h8 + h:h8 + h + 1, lanes]) * inv_z
                eb_s[h, tc] = jnp.exp(s2_ref[h, tc] - r_ref[2 * h8 + h:2 * h8 + h + 1, lanes])
        acc_s[...] = jnp.zeros_like(acc_s)

    hpre = _nt_dot(u_ref[...], xn_ref[...])
    grp = pl.ds(pl.multiple_of(j * SUBLANES, SUBLANES), SUBLANES)
    for c in range(SUBLANES):
        for tc in range(tm // LANES):
            lanes = slice(tc * LANES, (tc + 1) * LANES)
            s1row = [s1_ref[h, tc, grp, :][c:c + 1] for h in range(h8)]
            earow = [ea_s[h, tc, grp, :][c:c + 1] for h in range(h8)]
            for rb in range(N_KEYS // GATE_ROWS):
                keys = slice(rb * GATE_ROWS, (rb + 1) * GATE_ROWS)
                w = None
                for h in range(h8):
                    pair = s2_ref[h, tc, keys, :] + s1row[h]
                    gate = eb_s[h, tc, keys, :] * earow[h]
                    term = jnp.where(pair >= r_ref[h:h + 1, lanes], gate, 0.0)
                    w = term if w is None else w + term
                r0 = c * N_KEYS + rb * GATE_ROWS
                g_s[r0:r0 + GATE_ROWS, lanes] = (w * jax.nn.gelu(hpre[r0:r0 + GATE_ROWS, lanes])).astype(BF16)
    acc_s[...] += _dot(vt_ref[...], g_s[...])

    @pl.when(j == pl.num_programs(1) - 1)
    def _():
        y = x_ref[...] + acc_s[...].T
        if final_norm:
            y = _rms(y, gf_ref[...])
        o_ref[...] = y


def peer_dense(xn, x, s1, s2, r, u, vt, g_final, tm, final_norm):
    m, d = x.shape
    tm = min(tm, m)
    n_exp = u.shape[0]
    te = SUBLANES * N_KEYS
    s_block = (PEER_HEADS, tm // LANES, N_KEYS, LANES)
    s_spec = pl.BlockSpec(s_block, lambda i, j: (0, i, 0, 0))
    rows = pl.BlockSpec((tm, d), lambda i, j: (i, 0))
    return pl.pallas_call(
        functools.partial(_peer_dense_kernel, tm=tm, final_norm=final_norm),
        grid=(m // tm, n_exp // te),
        in_specs=[rows, rows, s_spec, s_spec,
                  pl.BlockSpec((4 * PEER_HEADS, tm), lambda i, j: (0, i)),
                  pl.BlockSpec((te, d), lambda i, j: (j, 0)),
                  pl.BlockSpec((d, te), lambda i, j: (0, j)),
                  _const_spec((1, d))],
        out_specs=rows,
        out_shape=jax.ShapeDtypeStruct((m, d), F32),
        scratch_shapes=[pltpu.VMEM(s_block, F32), pltpu.VMEM(s_block, F32),
                        pltpu.VMEM((d, tm), F32), pltpu.VMEM((te, tm), BF16)],
        compiler_params=_cparams("parallel", "arbitrary"),
        name="peer_dense",
    )(xn, x, s1, s2, r, u, vt, g_final.reshape(1, d))


def peer_ffn(x, g, wq_hi, wq_lo, sk_hi, sk_lo, u, vt, g_final, final_norm):
    xn, s1, s2, r = peer_route(x, g, wq_hi, wq_lo, sk_hi, sk_lo, tm=256)
    return peer_dense(xn, x, s1, s2, r, u, vt, g_final, tm=512, final_norm=final_norm)


def _mla_proj_kernel(x_ref, gmix_ref, gkv_ref, wq_ref, wm_ref, wc_ref, wr_ref, wrs_ref, glat_ref, gq_ref,
                     wuq_ref, wuqs_ref, cos_ref, sin_ref, wuk_ref, wuv_ref,
                     qm_ref, c_ref, kr_ref, q_ref, *kv_refs, with_kv):
    x = x_ref[...]
    xhat = x * lax.rsqrt(jnp.mean(x * x, axis=-1, keepdims=True) + NORM_EPS)
    xn = (xhat * gmix_ref[...]).astype(BF16)
    xk = (xhat * gkv_ref[...]).astype(BF16)
    cos, sin = cos_ref[...], sin_ref[...]

    qm_ref[...] = _dot(xn, wm_ref[...]).astype(qm_ref.dtype)
    c = _rms(_dot(xk, wc_ref[...]), glat_ref[...])
    c_ref[...] = c
    kr = _dot(xk, wr_ref[...]) * cos + _dot(xk, wrs_ref[...]) * sin
    kr_ref[...] = kr

    qn = _rms(_dot(xn, wq_ref[...]), gq_ref[...]).astype(BF16)
    qa = _dot(qn, wuq_ref[...])
    qb = _dot(qn, wuqs_ref[...])
    c_bf = c.astype(BF16)
    if with_kv:
        k_ref, v_ref = kv_refs
        kn = _dot(c_bf, wuk_ref[...])
        v_ref[...] = _dot(c_bf, wuv_ref[...]).astype(v_ref.dtype)
    for h in range(MLA_HEADS):
        cols = slice(h * HEAD_PAD, (h + 1) * HEAD_PAD)
        q_ref[:, cols] = (qa[:, cols] * cos + qb[:, cols] * sin).astype(q_ref.dtype)
        if with_kv:
            k_ref[:, cols] = (kn[:, cols] + kr).astype(k_ref.dtype)


def mla_proj(x, cos_t, sin_t, w, tm, with_kv):
    m, d = x.shape
    tm = min(tm, m)
    rows = lambda n: pl.BlockSpec((tm, n), lambda i: (i, 0))
    consts = [w["g_mix"], w["g_kv"], w["w_q"], w["w_m"], w["w_c"], w["w_r"], w["w_rs"], w["g_lat"], w["g_q"],
              w["w_uq"], w["w_uqs"]]
    tail = [w["w_uk"], w["w_uv"]]
    hp = MLA_HEADS * HEAD_PAD
    out_specs = [rows(MEM_W), rows(KV_LORA), rows(HEAD_PAD), rows(hp)]
    out_shape = [jax.ShapeDtypeStruct((m, MEM_W), BF16), jax.ShapeDtypeStruct((m, KV_LORA), F32),
                 jax.ShapeDtypeStruct((m, HEAD_PAD), F32), jax.ShapeDtypeStruct((m, hp), BF16)]
    if with_kv:
        out_specs += [rows(hp), rows(MLA_HEADS * V_DIM)]
        out_shape += [jax.ShapeDtypeStruct((m, hp), BF16), jax.ShapeDtypeStruct((m, MLA_HEADS * V_DIM), BF16)]
    return pl.pallas_call(
        functools.partial(_mla_proj_kernel, with_kv=with_kv),
        grid=(m // tm,),
        in_specs=[rows(d)] + [_const_spec(a.shape) for a in consts] + [rows(HEAD_PAD), rows(HEAD_PAD)]
        + [_const_spec(a.shape) for a in tail],
        out_specs=out_specs,
        out_shape=out_shape,
        compiler_params=_cparams("parallel"),
        name="mla_proj",
    )(x, *consts, cos_t, sin_t, *tail)


def _flash_kernel(q_ref, k_ref, v_ref, o_ref, *, bq, bk):
    qi = pl.program_id(2)
    row = lax.broadcasted_iota(jnp.int32, (bq, bk), 0)
    col = lax.broadcasted_iota(jnp.int32, (bq, bk), 1)
    outs = []
    for hh in range(2):
        cols = slice(hh * HEAD_PAD, (hh + 1) * HEAD_PAD)
        q = q_ref[0, :, cols]

        def block(j, carry, masked):
            m, l, acc = carry
            ks = pl.ds(pl.multiple_of(j * bk, bk), bk)
            s = _nt_dot(q, k_ref[0, ks, cols])
            if masked:
                s = jnp.where(col <= row, s, -jnp.inf)
            m_new = jnp.maximum(m, jnp.max(s, axis=-1, keepdims=True))
            alpha = jnp.exp2((m - m_new) * SCALE_LOG2E)
            p = jnp.exp2((s - m_new) * SCALE_LOG2E)
            l = alpha * l + jnp.sum(p, axis=-1, keepdims=True)
            acc = alpha * acc + _dot(p.astype(BF16), v_ref[0, ks, :])
            return m_new, l, acc

        init = (jnp.full((bq, 1), -jnp.inf, F32), jnp.zeros((bq, 1), F32), jnp.zeros((bq, 2 * V_DIM), F32))
        carry = lax.fori_loop(0, qi, functools.partial(block, masked=False), init)
        m, l, acc = block(qi, carry, masked=True)
        outs.append(acc / l)
    lane = lax.broadcasted_iota(jnp.int32, (bq, 2 * V_DIM), 1)
    o_ref[0] = jnp.where(lane < V_DIM, outs[0], outs[1]).astype(o_ref.dtype)


def flash_mla(q, k, v, bq):
    b, t, _ = q.shape
    bq = min(bq, t)
    return pl.pallas_call(
        functools.partial(_flash_kernel, bq=bq, bk=bq),
        grid=(b, MLA_HEADS // 2, t // bq),
        in_specs=[pl.BlockSpec((1, bq, 2 * HEAD_PAD), lambda b_, h, i: (b_, i, h)),
                  pl.BlockSpec((1, t, 2 * HEAD_PAD), lambda b_, h, i: (b_, 0, h)),
                  pl.BlockSpec((1, t, 2 * V_DIM), lambda b_, h, i: (b_, 0, h))],
        out_specs=pl.BlockSpec((1, bq, 2 * V_DIM), lambda b_, h, i: (b_, i, h)),
        out_shape=jax.ShapeDtypeStruct((b, t, MLA_HEADS * V_DIM), BF16),
        compiler_params=_cparams("parallel", "parallel", "arbitrary"),
        name="flash_mla",
    )(q, k, v)


def _bmm_kernel(a_ref, b_ref, o_ref):
    o_ref[0] = _dot(a_ref[0], b_ref[0]).astype(o_ref.dtype)


def head_matmul(a, b, out_dtype):
    h, m, k = a.shape
    n = b.shape[2]
    return pl.pallas_call(
        _bmm_kernel,
        grid=(h,),
        in_specs=[pl.BlockSpec((1, m, k), lambda i: (i, 0, 0)), pl.BlockSpec((1, k, n), lambda i: (i, 0, 0))],
        out_specs=pl.BlockSpec((1, m, n), lambda i: (i, 0, 0)),
        out_shape=jax.ShapeDtypeStruct((h, m, n), out_dtype),
        compiler_params=_cparams("parallel"),
        name="head_matmul",
    )(a, b)


def _paged_kernel(pt_ref, ql_ref, qr_ref, cn_ref, kn_ref, lat_hbm, rope_hbm, o_ref,
                  cbuf, rbuf, sem, *, n_chunks, chunk_pages, page, steps):
    b = pl.program_id(0)
    nb = pl.num_programs(0)

    def copies(bb, c, slot):
        out = []
        for p in range(chunk_pages):
            pid = pt_ref[bb, c * chunk_pages + p]
            dst = pl.ds(p * page, page)
            out.append(pltpu.make_async_copy(lat_hbm.at[pid], cbuf.at[slot, dst], sem.at[slot, 0]))
            out.append(pltpu.make_async_copy(rope_hbm.at[pid], rbuf.at[slot, :, dst], sem.at[slot, 1]))
        return out

    def start(bb, c, slot):
        for cp in copies(bb, c, slot):
            cp.start()

    @pl.when(b == 0)
    def _():
        start(0, 0, 0)

    ql = ql_ref[0]
    qr = qr_ref[0]
    n_rows = ql.shape[0]

    def merge(carry, s, values):
        m, l, acc = carry
        m_new = jnp.maximum(m, jnp.max(s, axis=-1, keepdims=True))
        alpha = jnp.exp2((m - m_new) * SCALE_LOG2E)
        p = jnp.exp2((s - m_new) * SCALE_LOG2E)
        l = alpha * l + jnp.sum(p, axis=-1, keepdims=True)
        acc = alpha * acc + _dot(p.astype(BF16), values)
        return m_new, l, acc

    def chunk(c, carry):
        g = b * n_chunks + c
        slot = lax.rem(g, 2)
        last = c == n_chunks - 1

        @pl.when(jnp.logical_not(last))
        def _():
            start(b, c + 1, 1 - slot)

        @pl.when(jnp.logical_and(last, b + 1 < nb))
        def _():
            start(b + 1, 0, 1 - slot)

        for cp in copies(b, c, slot):
            cp.wait()
        lat = cbuf[slot].astype(BF16)
        kr_t = rbuf[slot].astype(BF16)
        return merge(carry, _nt_dot(ql, lat) + _dot(qr, kr_t), lat)

    init = (jnp.full((n_rows, 1), -jnp.inf, F32), jnp.zeros((n_rows, 1), F32), jnp.zeros((n_rows, KV_LORA), F32))
    carry = lax.fori_loop(0, n_chunks, chunk, init)

    cn = cn_ref[0].astype(BF16)
    s = _nt_dot(ql, cn) + _nt_dot(qr, kn_ref[0].astype(BF16))
    t_row = lax.broadcasted_iota(jnp.int32, s.shape, 0) // MLA_HEADS
    t_col = lax.broadcasted_iota(jnp.int32, s.shape, 1)
    s = jnp.where(jnp.logical_and(t_col <= t_row, t_col < steps), s, -jnp.inf)
    m, l, acc = merge(carry, s, cn)
    o_ref[0] = acc / l


def paged_mla(page_table, ql, qr, c_new, kr_new, cache_latent, cache_k_rope_t, steps, chunk_pages):
    bsz, n_rows, _ = ql.shape
    n_pages = page_table.shape[1]
    page = cache_latent.shape[1]
    chunk_pages = min(chunk_pages, n_pages)
    n_chunks = n_pages // chunk_pages
    per_b = lambda shape: pl.BlockSpec((1,) + shape, lambda b, pt: (b, 0, 0))
    grid_spec = pltpu.PrefetchScalarGridSpec(
        num_scalar_prefetch=1,
        grid=(bsz,),
        in_specs=[per_b((n_rows, KV_LORA)), per_b((n_rows, QK_ROPE)),
                  per_b((NEW_ROWS, KV_LORA)), per_b((NEW_ROWS, QK_ROPE)),
                  pl.BlockSpec(memory_space=pl.ANY), pl.BlockSpec(memory_space=pl.ANY)],
        out_specs=per_b((n_rows, KV_LORA)),
        scratch_shapes=[pltpu.VMEM((2, chunk_pages * page, KV_LORA), F32),
                        pltpu.VMEM((2, QK_ROPE, chunk_pages * page), F32),
                        pltpu.SemaphoreType.DMA((2, 2))],
    )
    return pl.pallas_call(
        functools.partial(_paged_kernel, n_chunks=n_chunks, chunk_pages=chunk_pages, page=page, steps=steps),
        grid_spec=grid_spec,
        out_shape=jax.ShapeDtypeStruct((bsz, n_rows, KV_LORA), F32),
        compiler_params=_cparams("arbitrary"),
        name="paged_mla",
    )(page_table, ql, qr, c_new, kr_new, cache_latent, cache_k_rope_t)


def _pad_heads(w, width):
    k, h, _ = w.shape
    return jnp.pad(w, ((0, 0), (0, 0), (0, HEAD_PAD - width))).reshape(k, h * HEAD_PAD)


def _rope_swap(w):
    half = QK_ROPE // 2
    return jnp.concatenate([w[..., half:], w[..., :half]], axis=-1)


def _rope_tables(pos):
    half = QK_ROPE // 2
    freqs = ROPE_BASE ** (-jnp.arange(half, dtype=F32) / half)
    ang = pos.astype(F32)[:, None] * freqs
    cos, sin = jnp.cos(ang), jnp.sin(ang)
    n = pos.shape[0]
    tail = jnp.zeros((n, HEAD_PAD - QK_NOPE - QK_ROPE), F32)
    cos_t = jnp.concatenate([jnp.ones((n, QK_NOPE), F32), cos, cos, tail], axis=-1)
    sin_t = jnp.concatenate([jnp.zeros((n, QK_NOPE), F32), -sin, sin, tail], axis=-1)
    return cos_t, sin_t


def _prep_weights(p):
    bf = lambda a: a.astype(BF16)
    d = p["a_w_in"].shape[1]
    w = {}
    a_in = p["a_w_in"][0]
    d_rnn = p["a_conv_w"].shape[2]
    w["a_in"] = [bf(a_in[:, :d_rnn]), bf(a_in[:, d_rnn:2 * d_rnn]), bf(a_in[:, 2 * d_rnn:])]
    w["a_wri"] = bf(jnp.concatenate([p["a_gate_r_w"][0], p["a_gate_i_w"][0]], axis=-1))
    w["a_out"] = (bf(p["a_w_out"][0][:d_rnn]), bf(p["a_w_out"][0][d_rnn:]))
    b_in = p["b_w_in"][0]
    kv_a = p["w_kv_a"]
    zeros_n = jnp.zeros((d, QK_NOPE), F32)
    zeros_t = jnp.zeros((d, HEAD_PAD - QK_NOPE - QK_ROPE), F32)
    w_r = kv_a[:, KV_LORA:]
    uq = p["b_w_uq"][0].reshape(Q_LORA, MLA_HEADS, QK_NOPE + QK_ROPE)
    uq_n, uq_r = uq[..., :QK_NOPE], uq[..., QK_NOPE:]
    w["mla"] = {
        "g_mix": p["norm_mix"][1].reshape(1, d), "g_kv": p["kv_norm"].reshape(1, d),
        "w_q": bf(b_in[:, :Q_LORA]), "w_m": bf(b_in[:, Q_LORA:]),
        "w_c": bf(kv_a[:, :KV_LORA]),
        "w_r": bf(jnp.concatenate([zeros_n, w_r, zeros_t], axis=-1)),
        "w_rs": bf(jnp.concatenate([zeros_n, _rope_swap(w_r), zeros_t], axis=-1)),
        "g_lat": p["kv_latent_norm"].reshape(1, KV_LORA), "g_q": p["b_q_norm"][0].reshape(1, Q_LORA),
        "w_uq": bf(_pad_heads(jnp.concatenate([uq_n, uq_r], axis=-1), QK_NOPE + QK_ROPE)),
        "w_uqs": bf(_pad_heads(jnp.concatenate([jnp.zeros_like(uq_n), _rope_swap(uq_r)], axis=-1),
                               QK_NOPE + QK_ROPE)),
        "w_uk": bf(_pad_heads(p["w_uk"], QK_NOPE)),
        "w_uv": bf(p["w_uv"].reshape(KV_LORA, MLA_HEADS * V_DIM)),
    }
    w["uk_t"] = bf(jnp.transpose(p["w_uk"], (1, 2, 0)))
    w["uv_h"] = bf(jnp.transpose(p["w_uv"], (1, 0, 2)))
    hv = MLA_HEADS * V_DIM
    w["b_out"] = (bf(p["b_w_out"][0][:hv]), bf(p["b_w_out"][0][hv:]))
    w["peer"] = []
    for l in range(p["peer_wq"].shape[0]):
        wq_hi, wq_lo = _split_bf16(p["peer_wq"][l].T)
        sk_hi, sk_lo = _split_bf16(p["peer_sub_keys"][l])
        w["peer"].append((wq_hi, wq_lo, sk_hi, sk_lo, bf(p["peer_u"][l]), bf(p["peer_v"][l].T)))
    return w


def _run_group(x, pos, mem_k, mem_v, conv_state, h0, past, p, w):
    bsz, t, d = x.shape
    m = bsz * t
    xf = x.reshape(m, d)
    decode = past is not None
    tm = 512

    xb, gb, qm = norm_matmul(xf, p["norm_mix"][0], w["a_in"], (F32, F32, BF16), tm)
    c = xb.shape[1]
    xb3, gb3 = xb.reshape(bsz, t, c), gb.reshape(bsz, t, c)
    rg = (p["a_conv_w"][0], p["a_conv_b"][0], w["a_wri"], p["a_gate_r_b"][0], p["a_gate_i_b"][0], p["a_lambda"][0])
    if decode:
        ext = jnp.concatenate([jnp.swapaxes(conv_state, 0, 1), jnp.swapaxes(xb3, 0, 1)], axis=0)
        gated, h_last = rglru_step(ext, jnp.swapaxes(gb3, 0, 1), h0, *rg)
        gated = jnp.swapaxes(gated, 0, 1).reshape(m, c)
        new_conv = jnp.swapaxes(ext[t:], 0, 1)
    else:
        gated, h_last = rglru_seq(xb3, gb3, conv_state, h0, *rg, tt=512)
        gated = gated.reshape(m, c)
        h_last = h_last.reshape(bsz, c)
        new_conv = jnp.concatenate([conv_state, xb3], axis=1)[:, t:] if t < CONV_W - 1 else xb3[:, t - (CONV_W - 1):]
    n_mem = mem_k.shape[2]
    mk = mem_k.reshape(mem_k.shape[0], bsz, n_mem * MEM_HEADS, MEM_HD)
    mv = mem_v.reshape(mem_v.shape[0], bsz, n_mem * MEM_HEADS, MEM_HD)
    att = mem_attention(qm.reshape(bsz, t, MEM_W), mk, mv, 0, tt=512).reshape(m, MEM_W)
    xf = out_proj(xf, gated, att, *w["a_out"], tm)
    xf = peer_ffn(xf, p["norm_ffn"][0], *w["peer"][0], p["final_norm"], final_norm=False)

    cos_t, sin_t = _rope_tables(jnp.tile(pos, bsz))
    outs = mla_proj(xf, cos_t, sin_t, w["mla"], tm, with_kv=not decode)
    qm, c_kv, kr_pad, q_pad = outs[:4]
    k_rope = kr_pad[:, QK_NOPE:QK_NOPE + QK_ROPE]
    hv = MLA_HEADS * V_DIM
    if decode:
        cache_latent, cache_k_rope, page_table = past
        q4 = q_pad.reshape(m, MLA_HEADS, HEAD_PAD)
        q_nope = jnp.swapaxes(q4[..., :QK_NOPE], 0, 1)
        q_lat = head_matmul(q_nope, w["uk_t"], BF16)
        n_rows = t * MLA_HEADS
        q_lat = jnp.swapaxes(q_lat, 0, 1).reshape(bsz, n_rows, KV_LORA)
        q_rope = q4[..., QK_NOPE:QK_NOPE + QK_ROPE].reshape(bsz, n_rows, QK_ROPE)
        pad_rows = lambda a: jnp.pad(a.reshape(bsz, t, -1), ((0, 0), (0, NEW_ROWS - t), (0, 0)))
        o_lat = paged_mla(page_table, q_lat, q_rope, pad_rows(c_kv), pad_rows(k_rope),
                          cache_latent, jnp.swapaxes(cache_k_rope, 1, 2), steps=t, chunk_pages=PAGED_CHUNK_PAGES)
        o_lat = jnp.swapaxes(o_lat.reshape(m, MLA_HEADS, KV_LORA), 0, 1).astype(BF16)
        y_a = jnp.swapaxes(head_matmul(o_lat, w["uv_h"], BF16), 0, 1).reshape(m, hv)
    else:
        k_pad, v = outs[4:]
        hp = MLA_HEADS * HEAD_PAD
        y_a = flash_mla(q_pad.reshape(bsz, t, hp), k_pad.reshape(bsz, t, hp), v.reshape(bsz, t, hv), bq=1024)
        y_a = y_a.reshape(m, hv)
    att = mem_attention(qm.reshape(bsz, t, MEM_W), mk, mv, 1, tt=512).reshape(m, MEM_W)
    xf = out_proj(xf, y_a, att, *w["b_out"], tm)
    y = peer_ffn(xf, p["norm_ffn"][1], *w["peer"][1], p["final_norm"], final_norm=True)

    return (y.reshape(bsz, t, d), c_kv.reshape(bsz, t, KV_LORA), k_rope.reshape(bsz, t, QK_ROPE),
            new_conv[None], h_last[None])


def kernel(x_prompt, x_sample, cache_latent, cache_k_rope, state_conv, state_rglru, cache_mem_k, cache_mem_v,
           page_table, mem_prompt, norm_mix, norm_ffn, norm_mem, w_mem_kv, a_w_in, a_conv_w, a_conv_b,
           a_gate_r_w, a_gate_r_b, a_gate_i_w, a_gate_i_b, a_lambda, a_w_out, kv_norm, w_kv_a, kv_latent_norm,
           w_uk, w_uv, b_w_in, b_q_norm, b_w_uq, b_w_out, peer_wq, peer_sub_keys, peer_u, peer_v, final_norm):
    p = dict(norm_mix=norm_mix, norm_ffn=norm_ffn, a_w_in=a_w_in, a_conv_w=a_conv_w, a_conv_b=a_conv_b,
             a_gate_r_w=a_gate_r_w, a_gate_r_b=a_gate_r_b, a_gate_i_w=a_gate_i_w, a_gate_i_b=a_gate_i_b,
             a_lambda=a_lambda, a_w_out=a_w_out, kv_norm=kv_norm, w_kv_a=w_kv_a, kv_latent_norm=kv_latent_norm,
             w_uk=w_uk, w_uv=w_uv, b_w_in=b_w_in, b_q_norm=b_q_norm, b_w_uq=b_w_uq, b_w_out=b_w_out,
             peer_wq=peer_wq, peer_sub_keys=peer_sub_keys, peer_u=peer_u, peer_v=peer_v, final_norm=final_norm)
    w = _prep_weights(p)
    depth = norm_mem.shape[0]

    b, t, d = x_prompt.shape
    n_mem = mem_prompt.shape[1]
    mem_flat = mem_prompt.reshape(b * n_mem, d)
    kvs = [norm_matmul(mem_flat, norm_mem[l], [w_mem_kv[l].astype(BF16)], (F32,), 512)[0] for l in range(depth)]
    mem_k_prompt = jnp.stack([kv[:, :MEM_W].reshape(b, n_mem, MEM_HEADS, MEM_HD) for kv in kvs])
    mem_v_prompt = jnp.stack([kv[:, MEM_W:].reshape(b, n_mem, MEM_HEADS, MEM_HD) for kv in kvs])
    d_rnn = a_conv_w.shape[2]
    conv0 = jnp.zeros((b, CONV_W - 1, d_rnn), x_prompt.dtype)
    h0 = jnp.zeros((b, d_rnn), x_prompt.dtype)
    y_prompt, latent_prompt, k_rope_prompt, conv_prompt, rglru_prompt = _run_group(
        x_prompt, jnp.arange(t), mem_k_prompt, mem_v_prompt, conv0, h0, None, p, w)

    ts = x_sample.shape[1]
    past_len = page_table.shape[1] * cache_latent.shape[1]
    y_sample, latent_sample, k_rope_sample, conv_sample, rglru_sample = _run_group(
        x_sample, past_len + jnp.arange(ts), cache_mem_k, cache_mem_v, state_conv[0], state_rglru[0],
        (cache_latent, cache_k_rope, page_table), p, w)

    return (y_prompt, y_sample, latent_prompt, k_rope_prompt, conv_prompt, rglru_prompt,
            mem_k_prompt, mem_v_prompt, latent_sample, k_rope_sample, conv_sample, rglru_sample)
```

```python
import functools
import math

import jax
import jax.numpy as jnp
from jax import lax
from jax.experimental import pallas as pl
from jax.experimental.pallas import tpu as pltpu

F32 = jnp.float32
BF16 = jnp.bfloat16

NORM_EPS = 1e-6
RG_C = 8.0
RNN_BLOCKS = 8
CONV_W = 4
MLA_HEADS = 16
QK_NOPE = 64
QK_ROPE = 32
V_DIM = 64
KV_LORA = 256
Q_LORA = 384
ROPE_BASE = 10000.0
MLA_SCALE = (QK_NOPE + QK_ROPE) ** -0.5
SCALE_LOG2E = MLA_SCALE * math.log2(math.e)
MEM_HEADS = 4
MEM_HD = 128
MEM_W = MEM_HEADS * MEM_HD
PEER_HEADS = 8
N_KEYS = 128
PEER_HALF = 64
PEER_TOPK = 16

LANES = 128
SUBLANES = 8
HEAD_PAD = LANES
NEW_ROWS = 16
PAGED_CHUNK_PAGES = 32
GATE_ROWS = 32
PEER_TE = SUBLANES * N_KEYS
VMEM_LIMIT = 56 * 1024 * 1024


def _cparams(*sem):
    return pltpu.CompilerParams(dimension_semantics=sem, vmem_limit_bytes=VMEM_LIMIT)


def _nt_dot(a, b):
    return lax.dot_general(a, b, (((1,), (1,)), ((), ())), preferred_element_type=F32)


def _dot(a, b):
    return jnp.dot(a, b, preferred_element_type=F32)


def _rms(x, g):
    return x * lax.rsqrt(jnp.mean(x * x, axis=-1, keepdims=True) + NORM_EPS) * g


def _split_bf16(x):
    hi = x.astype(BF16)
    lo = (x - hi.astype(F32)).astype(BF16)
    return hi, lo


def _const_spec(shape):
    nd = len(shape)
    return pl.BlockSpec(shape, lambda *_: (0,) * nd)


def _norm_matmul_kernel(x_ref, g_ref, *refs, n_out):
    xn = _rms(x_ref[...], g_ref[...]).astype(BF16)
    for w_ref, o_ref in zip(refs[:n_out], refs[n_out:]):
        o_ref[...] = _dot(xn, w_ref[...]).astype(o_ref.dtype)


def norm_matmul(x, g, ws, out_dtypes, tm):
    m, k = x.shape
    tm = min(tm, m)
    return pl.pallas_call(
        functools.partial(_norm_matmul_kernel, n_out=len(ws)),
        grid=(m // tm,),
        in_specs=[pl.BlockSpec((tm, k), lambda i: (i, 0)), _const_spec((1, k))]
        + [_const_spec(w.shape) for w in ws],
        out_specs=[pl.BlockSpec((tm, w.shape[1]), lambda i: (i, 0)) for w in ws],
        out_shape=[jax.ShapeDtypeStruct((m, w.shape[1]), dt) for w, dt in zip(ws, out_dtypes)],
        compiler_params=_cparams("parallel"),
        name="norm_matmul",
    )(x, g.reshape(1, k), *ws)


def _rglru_gates(conv, wri_ref, br, bi, lam):
    bw = conv.shape[1] // RNN_BLOCKS
    conv_bf = conv.astype(BF16)
    rs, gs = [], []
    for n in range(RNN_BLOCKS):
        ri = _dot(conv_bf[:, n * bw:(n + 1) * bw], wri_ref[n])
        rs.append(ri[:, :bw])
        gs.append(ri[:, bw:])
    r = jax.nn.sigmoid(jnp.concatenate(rs, axis=-1) + br)
    i = jax.nn.sigmoid(jnp.concatenate(gs, axis=-1) + bi)
    z = -lam
    softplus = jnp.maximum(z, 0.0) + jnp.log1p(jnp.exp(-jnp.abs(z)))
    log_a = -RG_C * r * softplus
    a = jnp.exp(log_a)
    b = jnp.sqrt(-jnp.tanh(log_a) * (a * a + 1.0)) * (i * conv)
    return a, b


def _rglru_seq_kernel(xb_ref, gb_ref, cs_ref, h0_ref, cw_ref, cb_ref, wri_ref, br_ref, bi_ref, lam_ref,
                      o_ref, hl_ref, ext_s, a_s, b_s, h_s, *, tt):
    t = pl.program_id(1)
    pad = SUBLANES - (CONV_W - 1)

    @pl.when(t == 0)
    def _():
        ext_s[pad:SUBLANES, :] = cs_ref[0]
        h_s[...] = h0_ref[0]

    xb = xb_ref[0]
    ext_s[SUBLANES:SUBLANES + tt, :] = xb
    conv = cb_ref[...] + xb * cw_ref[CONV_W - 1:CONV_W, :]
    for k in range(CONV_W - 1):
        conv = conv + ext_s[pad + k:pad + k + tt, :] * cw_ref[k:k + 1, :]
    ext_s[pad:SUBLANES, :] = ext_s[tt + pad:tt + SUBLANES, :]

    a, b = _rglru_gates(conv, wri_ref, br_ref[...], bi_ref[...], lam_ref[...])
    a_s[...] = a
    b_s[...] = b

    sub = lax.broadcasted_iota(jnp.int32, (SUBLANES, a.shape[1]), 0)

    def rows(g, h):
        grp = pl.ds(pl.multiple_of(g * SUBLANES, SUBLANES), SUBLANES)
        a8, b8 = a_s[grp, :], b_s[grp, :]
        h8 = jnp.zeros_like(a8)
        for j in range(SUBLANES):
            h = a8[j:j + 1] * h + b8[j:j + 1]
            h8 = jnp.where(sub == j, h, h8)
        b_s[grp, :] = h8
        return h

    h = lax.fori_loop(0, tt // SUBLANES, rows, h_s[...])
    h_s[...] = h
    hl_ref[0] = h
    o_ref[0] = (b_s[...] * jax.nn.gelu(gb_ref[0])).astype(o_ref.dtype)


def rglru_seq(xb, gb, conv_state, h0, cw, cb, wri, br, bi, lam, tt):
    bsz, t, c = xb.shape
    tt = min(tt, t)
    row = lambda v: v.reshape(1, c)
    tile = pl.BlockSpec((1, tt, c), lambda b, i: (b, i, 0))
    return pl.pallas_call(
        functools.partial(_rglru_seq_kernel, tt=tt),
        grid=(bsz, t // tt),
        in_specs=[tile, tile,
                  pl.BlockSpec((1, CONV_W - 1, c), lambda b, i: (b, 0, 0)),
                  pl.BlockSpec((1, 1, c), lambda b, i: (b, 0, 0)),
                  _const_spec((CONV_W, c)), _const_spec((1, c)), _const_spec(wri.shape),
                  _const_spec((1, c)), _const_spec((1, c)), _const_spec((1, c))],
        out_specs=[tile, pl.BlockSpec((1, 1, c), lambda b, i: (b, 0, 0))],
        out_shape=[jax.ShapeDtypeStruct((bsz, t, c), BF16), jax.ShapeDtypeStruct((bsz, 1, c), F32)],
        scratch_shapes=[pltpu.VMEM((tt + SUBLANES, c), F32), pltpu.VMEM((tt, c), F32),
                        pltpu.VMEM((tt, c), F32), pltpu.VMEM((1, c), F32)],
        compiler_params=_cparams("parallel", "arbitrary"),
        name="rglru_seq",
    )(xb, gb, conv_state, h0.reshape(bsz, 1, c), cw, row(cb), wri, row(br), row(bi), row(lam))


def _rglru_step_kernel(ext_ref, gb_ref, h0_ref, cw_ref, cb_ref, wri_ref, br_ref, bi_ref, lam_ref,
                       o_ref, hl_ref, *, steps):
    h = h0_ref[...]
    for t in range(steps):
        conv = cb_ref[...]
        for k in range(CONV_W):
            conv = conv + ext_ref[t + k] * cw_ref[k:k + 1, :]
        a, b = _rglru_gates(conv, wri_ref, br_ref[...], bi_ref[...], lam_ref[...])
        h = a * h + b
        o_ref[t] = (h * jax.nn.gelu(gb_ref[t])).astype(o_ref.dtype)
    hl_ref[...] = h


def rglru_step(ext, gb, h0, cw, cb, wri, br, bi, lam):
    steps, bsz, c = gb.shape
    row = lambda v: v.reshape(1, c)
    return pl.pallas_call(
        functools.partial(_rglru_step_kernel, steps=steps),
        out_shape=[jax.ShapeDtypeStruct((steps, bsz, c), BF16), jax.ShapeDtypeStruct((bsz, c), F32)],
        compiler_params=pltpu.CompilerParams(vmem_limit_bytes=VMEM_LIMIT),
        name="rglru_step",
    )(ext, gb, h0, cw, row(cb), wri, row(br), row(bi), row(lam))


def _mem_attn_kernel(q_ref, k_ref, v_ref, o_ref):
    q = q_ref[0]
    k = k_ref[0, 0].astype(BF16)
    v = v_ref[0, 0].astype(BF16)
    row_head = lax.broadcasted_iota(jnp.int32, (q.shape[0], k.shape[0]), 1) % MEM_HEADS
    outs = []
    for h in range(MEM_HEADS):
        s = _nt_dot(q[:, h * MEM_HD:(h + 1) * MEM_HD], k) * (MEM_HD ** -0.5)
        s = jnp.where(row_head == h, s, -jnp.inf)
        p = jnp.exp(s - jnp.max(s, axis=-1, keepdims=True))
        l = jnp.sum(p, axis=-1, keepdims=True)
        outs.append(_dot(p.astype(BF16), v) / l)
    o_ref[0] = jnp.concatenate(outs, axis=-1).astype(o_ref.dtype)


def mem_attention(q, mk, mv, layer, tt):
    g, t_in, w = q.shape
    n_rows = mk.shape[2]
    t = -(-t_in // NEW_ROWS) * NEW_ROWS
    q = jnp.pad(q, ((0, 0), (0, t - t_in), (0, 0)))
    tt = min(tt, t)
    kv_spec = pl.BlockSpec((1, 1, n_rows, MEM_HD), lambda b, i: (layer, b, 0, 0))
    tile = pl.BlockSpec((1, tt, w), lambda b, i: (b, i, 0))
    return pl.pallas_call(
        _mem_attn_kernel,
        grid=(g, t // tt),
        in_specs=[tile, kv_spec, kv_spec],
        out_specs=tile,
        out_shape=jax.ShapeDtypeStruct((g, t, w), BF16),
        compiler_params=_cparams("parallel", "parallel"),
        name="mem_attention",
    )(q, mk, mv)[:, :t_in]


def _out_proj_kernel(x_ref, a_ref, b_ref, wa_ref, wb_ref, o_ref):
    o_ref[...] = x_ref[...] + (_dot(a_ref[...], wa_ref[...]) + _dot(b_ref[...], wb_ref[...]))


def out_proj(x, a, b, wa, wb, tm):
    m, d = x.shape
    tm = min(tm, m)
    rows = lambda n: pl.BlockSpec((tm, n), lambda i: (i, 0))
    return pl.pallas_call(
        _out_proj_kernel,
        grid=(m // tm,),
        in_specs=[rows(d), rows(a.shape[1]), rows(b.shape[1]), _const_spec(wa.shape), _const_spec(wb.shape)],
        out_specs=rows(d),
        out_shape=jax.ShapeDtypeStruct((m, d), F32),
        compiler_params=_cparams("parallel"),
        name="out_proj",
    )(x, a, b, wa, wb)


def _extract_top(vals, weights, n_top):
    out_v, out_n = [], []
    for it in range(n_top):
        m = functools.reduce(jnp.maximum, vals)
        eq = [v == m for v in vals]
        n = functools.reduce(jnp.add, [jnp.where(e, w, 0.0) for e, w in zip(eq, weights)])
        out_v.append(m)
        out_n.append(n)
        if it + 1 < n_top:
            vals = [jnp.where(e, -jnp.inf, v) for e, v in zip(eq, vals)]
    return out_v, out_n


def _peer_route_kernel(x_ref, g_ref, wq_hi_ref, wq_lo_ref, sk_hi_ref, sk_lo_ref,
                       xn_ref, s1_ref, s2_ref, r_ref, q_s, top_s, cnt_s, *, tm):
    xn = _rms(x_ref[...], g_ref[...])
    xn_ref[...] = xn.astype(BF16)
    x_hi, x_lo = _split_bf16(xn)
    q_s[...] = (_nt_dot(wq_hi_ref[...], x_hi) + _nt_dot(wq_hi_ref[...], x_lo)) + _nt_dot(wq_lo_ref[...], x_hi)

    for h in range(PEER_HEADS):
        for p, s_ref in enumerate((s1_ref, s2_ref)):
            base = (h * 2 + p) * PEER_HALF
            q_hi, q_lo = _split_bf16(q_s[base:base + PEER_HALF, :])
            s = (_dot(sk_hi_ref[p], q_hi) + _dot(sk_hi_ref[p], q_lo)) + _dot(sk_lo_ref[p], q_hi)
            for tc in range(tm // LANES):
                s_ref[h, tc] = s[:, tc * LANES:(tc + 1) * LANES]
            for it in range(PEER_TOPK):
                m = jnp.max(s, axis=0, keepdims=True)
                eq = s == m
                top_s[p, it, h:h + 1, :] = m
                cnt_s[p, it, h:h + 1, :] = jnp.sum(jnp.where(eq, 1.0, 0.0), axis=0, keepdims=True)
                if it + 1 < PEER_TOPK:
                    s = jnp.where(eq, -jnp.inf, s)

    for c in range(tm // LANES):
        lanes = slice(c * LANES, (c + 1) * LANES)
        a = [top_s[0, i, :, lanes] for i in range(PEER_TOPK)]
        b = [top_s[1, i, :, lanes] for i in range(PEER_TOPK)]
        na = [cnt_s[0, i, :, lanes] for i in range(PEER_TOPK)]
        nb = [cnt_s[1, i, :, lanes] for i in range(PEER_TOPK)]
        pairs = [(i, j) for i in range(PEER_TOPK) for j in range(PEER_TOPK) if (i + 1) * (j + 1) <= PEER_TOPK]
        vals = [a[i] + b[j] for i, j in pairs]
        mult = [na[i] * nb[j] for i, j in pairs]
        v, n = _extract_top(vals, mult, PEER_TOPK)
        thr = v[0]
        z = jnp.zeros_like(v[0])
        seen = jnp.zeros_like(v[0])
        for vk, nk in zip(v, n):
            live = seen < PEER_TOPK
            thr = jnp.where(live, vk, thr)
            z = z + jnp.where(live, jnp.minimum(nk, PEER_TOPK - seen) * jnp.exp(vk - v[0]), 0.0)
            seen = seen + nk
        h8 = PEER_HEADS
        r_ref[0 * h8:1 * h8, lanes] = thr
        r_ref[1 * h8:2 * h8, lanes] = a[0]
        r_ref[2 * h8:3 * h8, lanes] = b[0]
        r_ref[3 * h8:4 * h8, lanes] = 1.0 / z


def peer_route(x, g, wq_hi, wq_lo, sk_hi, sk_lo, tm):
    m, d = x.shape
    tm = min(tm, m)
    qw = wq_hi.shape[0]
    s_spec = pl.BlockSpec((PEER_HEADS, tm // LANES, N_KEYS, LANES), lambda i: (0, i, 0, 0))
    s_shape = jax.ShapeDtypeStruct((PEER_HEADS, m // LANES, N_KEYS, LANES), F32)
    return pl.pallas_call(
        functools.partial(_peer_route_kernel, tm=tm),
        grid=(m // tm,),
        in_specs=[pl.BlockSpec((tm, d), lambda i: (i, 0)), _const_spec((1, d)),
                  _const_spec(wq_hi.shape), _const_spec(wq_lo.shape),
                  _const_spec(sk_hi.shape), _const_spec(sk_lo.shape)],
        out_specs=[pl.BlockSpec((tm, d), lambda i: (i, 0)), s_spec, s_spec,
                   pl.BlockSpec((4 * PEER_HEADS, tm), lambda i: (0, i))],
        out_shape=[jax.ShapeDtypeStruct((m, d), BF16), s_shape, s_shape,
                   jax.ShapeDtypeStruct((4 * PEER_HEADS, m), F32)],
        scratch_shapes=[pltpu.VMEM((qw, tm), F32),
                        pltpu.VMEM((2, PEER_TOPK, PEER_HEADS, tm), F32),
                        pltpu.VMEM((2, PEER_TOPK, PEER_HEADS, tm), F32)],
        compiler_params=_cparams("parallel"),
        name="peer_route",
    )(x, g.reshape(1, d), wq_hi, wq_lo, sk_hi, sk_lo)


def _peer_dense_kernel(xn_ref, x_ref, s1_ref, s2_ref, r_ref, ua_ref, ub_ref, vta_ref, vtb_ref, gf_ref, o_ref,
                       ea_s, eb_s, acc_s, g_s, *, tm, final_norm):
    j = pl.program_id(1)
    h8 = PEER_HEADS
    dh = ua_ref.shape[1]
    eh = vta_ref.shape[1]

    @pl.when(j == 0)
    def _():
        for h in range(h8):
            for tc in range(tm // LANES):
                lanes = slice(tc * LANES, (tc + 1) * LANES)
                inv_z = r_ref[3 * h8 + h:3 * h8 + h + 1, lanes]
                ea_s[h, tc] = jnp.exp(s1_ref[h, tc] - r_ref[h8 + h:h8 + h + 1, lanes]) * inv_z
                eb_s[h, tc] = jnp.exp(s2_ref[h, tc] - r_ref[2 * h8 + h:2 * h8 + h + 1, lanes])
        acc_s[...] = jnp.zeros_like(acc_s)

    hpre = _nt_dot(ua_ref[...], xn_ref[:, :dh]) + _nt_dot(ub_ref[...], xn_ref[:, dh:])
    grp = pl.ds(pl.multiple_of(j * SUBLANES, SUBLANES), SUBLANES)
    for c in range(SUBLANES):
        for tc in range(tm // LANES):
            lanes = slice(tc * LANES, (tc + 1) * LANES)
            s1row = [s1_ref[h, tc, grp, :][c:c + 1] for h in range(h8)]
            earow = [ea_s[h, tc, grp, :][c:c + 1] for h in range(h8)]
            for rb in range(N_KEYS // GATE_ROWS):
                keys = slice(rb * GATE_ROWS, (rb + 1) * GATE_ROWS)
                w = None
                for h in range(h8):
                    pair = s2_ref[h, tc, keys, :] + s1row[h]
                    gate = eb_s[h, tc, keys, :] * earow[h]
                    term = jnp.where(pair >= r_ref[h:h + 1, lanes], gate, 0.0)
                    w = term if w is None else w + term
                r0 = c * N_KEYS + rb * GATE_ROWS
                g_s[r0:r0 + GATE_ROWS, lanes] = (w * jax.nn.gelu(hpre[r0:r0 + GATE_ROWS, lanes])).astype(BF16)
    acc_s[...] += _dot(vta_ref[...], g_s[:eh, :]) + _dot(vtb_ref[...], g_s[eh:, :])

    @pl.when(j == pl.num_programs(1) - 1)
    def _():
        y = x_ref[...] + acc_s[...].T
        if final_norm:
            y = _rms(y, gf_ref[...])
        o_ref[...] = y


def peer_dense(xn, x, s1, s2, r, u, vt, g_final, tm, final_norm):
    m, d = x.shape
    tm = min(tm, m)
    te = PEER_TE
    n_steps = u.shape[1]
    u_half = lambda k: pl.BlockSpec((None, None, te, d // 2), lambda i, j: (k, j, 0, 0))
    vt_half = lambda k: pl.BlockSpec((None, None, d, te // 2), lambda i, j: (j, k, 0, 0))
    s_block = (PEER_HEADS, tm // LANES, N_KEYS, LANES)
    s_spec = pl.BlockSpec(s_block, lambda i, j: (0, i, 0, 0))
    rows = pl.BlockSpec((tm, d), lambda i, j: (i, 0))
    return pl.pallas_call(
        functools.partial(_peer_dense_kernel, tm=tm, final_norm=final_norm),
        grid=(m // tm, n_steps),
        in_specs=[rows, rows, s_spec, s_spec,
                  pl.BlockSpec((4 * PEER_HEADS, tm), lambda i, j: (0, i)),
                  u_half(0), u_half(1), vt_half(0), vt_half(1),
                  _const_spec((1, d))],
        out_specs=rows,
        out_shape=jax.ShapeDtypeStruct((m, d), F32),
        scratch_shapes=[pltpu.VMEM(s_block, F32), pltpu.VMEM(s_block, F32),
                        pltpu.VMEM((d, tm), F32), pltpu.VMEM((te, tm), BF16)],
        compiler_params=_cparams("parallel", "arbitrary"),
        name="peer_dense",
    )(xn, x, s1, s2, r, u, u, vt, vt, g_final.reshape(1, d))


def peer_ffn(x, g, wq_hi, wq_lo, sk_hi, sk_lo, u, vt, g_final, final_norm):
    xn, s1, s2, r = peer_route(x, g, wq_hi, wq_lo, sk_hi, sk_lo, tm=256)
    return peer_dense(xn, x, s1, s2, r, u, vt, g_final, tm=512, final_norm=final_norm)


def _mla_proj_kernel(x_ref, gmix_ref, gkv_ref, wq_ref, wm_ref, wc_ref, wr_ref, wrs_ref, glat_ref, gq_ref,
                     wuq_ref, wuqs_ref, cos_ref, sin_ref, wuk_ref, wuv_ref,
                     qm_ref, c_ref, kr_ref, q_ref, *kv_refs, with_kv):
    x = x_ref[...]
    xhat = x * lax.rsqrt(jnp.mean(x * x, axis=-1, keepdims=True) + NORM_EPS)
    xn = (xhat * gmix_ref[...]).astype(BF16)
    xk = (xhat * gkv_ref[...]).astype(BF16)
    cos, sin = cos_ref[...], sin_ref[...]

    qm_ref[...] = _dot(xn, wm_ref[...]).astype(qm_ref.dtype)
    c = _rms(_dot(xk, wc_ref[...]), glat_ref[...])
    c_ref[...] = c
    kr = _dot(xk, wr_ref[...]) * cos + _dot(xk, wrs_ref[...]) * sin
    kr_ref[...] = kr

    qn = _rms(_dot(xn, wq_ref[...]), gq_ref[...]).astype(BF16)
    qa = _dot(qn, wuq_ref[...])
    qb = _dot(qn, wuqs_ref[...])
    c_bf = c.astype(BF16)
    if with_kv:
        k_ref, v_ref = kv_refs
        kn = _dot(c_bf, wuk_ref[...])
        v_ref[...] = _dot(c_bf, wuv_ref[...]).astype(v_ref.dtype)
    for h in range(MLA_HEADS):
        cols = slice(h * HEAD_PAD, (h + 1) * HEAD_PAD)
        q_ref[:, cols] = (qa[:, cols] * cos + qb[:, cols] * sin).astype(q_ref.dtype)
        if with_kv:
            k_ref[:, cols] = (kn[:, cols] + kr).astype(k_ref.dtype)


def mla_proj(x, cos_t, sin_t, w, tm, with_kv):
    m, d = x.shape
    tm = min(tm, m)
    rows = lambda n: pl.BlockSpec((tm, n), lambda i: (i, 0))
    consts = [w["g_mix"], w["g_kv"], w["w_q"], w["w_m"], w["w_c"], w["w_r"], w["w_rs"], w["g_lat"], w["g_q"],
              w["w_uq"], w["w_uqs"]]
    tail = [w["w_uk"], w["w_uv"]]
    hp = MLA_HEADS * HEAD_PAD
    out_specs = [rows(MEM_W), rows(KV_LORA), rows(HEAD_PAD), rows(hp)]
    out_shape = [jax.ShapeDtypeStruct((m, MEM_W), BF16), jax.ShapeDtypeStruct((m, KV_LORA), F32),
                 jax.ShapeDtypeStruct((m, HEAD_PAD), F32), jax.ShapeDtypeStruct((m, hp), BF16)]
    if with_kv:
        out_specs += [rows(hp), rows(MLA_HEADS * V_DIM)]
        out_shape += [jax.ShapeDtypeStruct((m, hp), BF16), jax.ShapeDtypeStruct((m, MLA_HEADS * V_DIM), BF16)]
    return pl.pallas_call(
        functools.partial(_mla_proj_kernel, with_kv=with_kv),
        grid=(m // tm,),
        in_specs=[rows(d)] + [_const_spec(a.shape) for a in consts] + [rows(HEAD_PAD), rows(HEAD_PAD)]
        + [_const_spec(a.shape) for a in tail],
        out_specs=out_specs,
        out_shape=out_shape,
        compiler_params=_cparams("parallel"),
        name="mla_proj",
    )(x, *consts, cos_t, sin_t, *tail)


def _flash_kernel(q_ref, k_ref, v_ref, o_ref, *, bq, bk):
    qi = pl.program_id(2)
    row = lax.broadcasted_iota(jnp.int32, (bq, bk), 0)
    col = lax.broadcasted_iota(jnp.int32, (bq, bk), 1)
    outs = []
    for hh in range(2):
        cols = slice(hh * HEAD_PAD, (hh + 1) * HEAD_PAD)
        q = q_ref[0, :, cols]

        def block(j, carry, masked):
            m, l, acc = carry
            ks = pl.ds(pl.multiple_of(j * bk, bk), bk)
            s = _nt_dot(q, k_ref[0, ks, cols])
            if masked:
                s = jnp.where(col <= row, s, -jnp.inf)
            m_new = jnp.maximum(m, jnp.max(s, axis=-1, keepdims=True))
            alpha = jnp.exp2((m - m_new) * SCALE_LOG2E)
            p = jnp.exp2((s - m_new) * SCALE_LOG2E)
            l = alpha * l + jnp.sum(p, axis=-1, keepdims=True)
            acc = alpha * acc + _dot(p.astype(BF16), v_ref[0, ks, :])
            return m_new, l, acc

        init = (jnp.full((bq, 1), -jnp.inf, F32), jnp.zeros((bq, 1), F32), jnp.zeros((bq, 2 * V_DIM), F32))
        carry = lax.fori_loop(0, qi, functools.partial(block, masked=False), init)
        m, l, acc = block(qi, carry, masked=True)
        outs.append(acc / l)
    lane = lax.broadcasted_iota(jnp.int32, (bq, 2 * V_DIM), 1)
    o_ref[0] = jnp.where(lane < V_DIM, outs[0], outs[1]).astype(o_ref.dtype)


def flash_mla(q, k, v, bq):
    b, t, _ = q.shape
    bq = min(bq, t)
    return pl.pallas_call(
        functools.partial(_flash_kernel, bq=bq, bk=bq),
        grid=(b, MLA_HEADS // 2, t // bq),
        in_specs=[pl.BlockSpec((1, bq, 2 * HEAD_PAD), lambda b_, h, i: (b_, i, h)),
                  pl.BlockSpec((1, t, 2 * HEAD_PAD), lambda b_, h, i: (b_, 0, h)),
                  pl.BlockSpec((1, t, 2 * V_DIM), lambda b_, h, i: (b_, 0, h))],
        out_specs=pl.BlockSpec((1, bq, 2 * V_DIM), lambda b_, h, i: (b_, i, h)),
        out_shape=jax.ShapeDtypeStruct((b, t, MLA_HEADS * V_DIM), BF16),
        compiler_params=_cparams("parallel", "parallel", "arbitrary"),
        name="flash_mla",
    )(q, k, v)


def _bmm_kernel(a_ref, b_ref, o_ref):
    o_ref[0] = _dot(a_ref[0], b_ref[0]).astype(o_ref.dtype)


def head_matmul(a, b, out_dtype):
    h, m, k = a.shape
    n = b.shape[2]
    return pl.pallas_call(
        _bmm_kernel,
        grid=(h,),
        in_specs=[pl.BlockSpec((1, m, k), lambda i: (i, 0, 0)), pl.BlockSpec((1, k, n), lambda i: (i, 0, 0))],
        out_specs=pl.BlockSpec((1, m, n), lambda i: (i, 0, 0)),
        out_shape=jax.ShapeDtypeStruct((h, m, n), out_dtype),
        compiler_params=_cparams("parallel"),
        name="head_matmul",
    )(a, b)


def _paged_kernel(pt_ref, ql_ref, qr_ref, cn_ref, kn_ref, lat_hbm, rope_hbm, o_ref,
                  cbuf, rbuf, sem, *, n_chunks, chunk_pages, page, steps):
    b = pl.program_id(0)
    nb = pl.num_programs(0)

    def copies(bb, c, slot):
        out = []
        for p in range(chunk_pages):
            pid = pt_ref[bb, c * chunk_pages + p]
            dst = pl.ds(p * page, page)
            out.append(pltpu.make_async_copy(lat_hbm.at[pid], cbuf.at[slot, dst], sem.at[slot, 0]))
            out.append(pltpu.make_async_copy(rope_hbm.at[pid], rbuf.at[slot, :, dst], sem.at[slot, 1]))
        return out

    def start(bb, c, slot):
        for cp in copies(bb, c, slot):
            cp.start()

    @pl.when(b == 0)
    def _():
        start(0, 0, 0)

    ql = ql_ref[0]
    qr = qr_ref[0]
    n_rows = ql.shape[0]

    def merge(carry, s, values):
        m, l, acc = carry
        m_new = jnp.maximum(m, jnp.max(s, axis=-1, keepdims=True))
        alpha = jnp.exp2((m - m_new) * SCALE_LOG2E)
        p = jnp.exp2((s - m_new) * SCALE_LOG2E)
        l = alpha * l + jnp.sum(p, axis=-1, keepdims=True)
        acc = alpha * acc + _dot(p.astype(BF16), values)
        return m_new, l, acc

    def chunk(c, carry):
        g = b * n_chunks + c
        slot = lax.rem(g, 2)
        last = c == n_chunks - 1

        @pl.when(jnp.logical_not(last))
        def _():
            start(b, c + 1, 1 - slot)

        @pl.when(jnp.logical_and(last, b + 1 < nb))
        def _():
            start(b + 1, 0, 1 - slot)

        for cp in copies(b, c, slot):
            cp.wait()
        lat = cbuf[slot].astype(BF16)
        kr_t = rbuf[slot].astype(BF16)
        return merge(carry, _nt_dot(ql, lat) + _dot(qr, kr_t), lat)

    init = (jnp.full((n_rows, 1), -jnp.inf, F32), jnp.zeros((n_rows, 1), F32), jnp.zeros((n_rows, KV_LORA), F32))
    carry = lax.fori_loop(0, n_chunks, chunk, init)

    cn = cn_ref[0].astype(BF16)
    s = _nt_dot(ql, cn) + _nt_dot(qr, kn_ref[0].astype(BF16))
    t_row = lax.broadcasted_iota(jnp.int32, s.shape, 0) // MLA_HEADS
    t_col = lax.broadcasted_iota(jnp.int32, s.shape, 1)
    s = jnp.where(jnp.logical_and(t_col <= t_row, t_col < steps), s, -jnp.inf)
    m, l, acc = merge(carry, s, cn)
    o_ref[0] = acc / l


def paged_mla(page_table, ql, qr, c_new, kr_new, cache_latent, cache_k_rope_t, steps, chunk_pages):
    bsz, n_rows, _ = ql.shape
    n_pages = page_table.shape[1]
    page = cache_latent.shape[1]
    chunk_pages = min(chunk_pages, n_pages)
    n_chunks = n_pages // chunk_pages
    per_b = lambda shape: pl.BlockSpec((1,) + shape, lambda b, pt: (b, 0, 0))
    grid_spec = pltpu.PrefetchScalarGridSpec(
        num_scalar_prefetch=1,
        grid=(bsz,),
        in_specs=[per_b((n_rows, KV_LORA)), per_b((n_rows, QK_ROPE)),
                  per_b((NEW_ROWS, KV_LORA)), per_b((NEW_ROWS, QK_ROPE)),
                  pl.BlockSpec(memory_space=pl.ANY), pl.BlockSpec(memory_space=pl.ANY)],
        out_specs=per_b((n_rows, KV_LORA)),
        scratch_shapes=[pltpu.VMEM((2, chunk_pages * page, KV_LORA), F32),
                        pltpu.VMEM((2, QK_ROPE, chunk_pages * page), F32),
                        pltpu.SemaphoreType.DMA((2, 2))],
    )
    return pl.pallas_call(
        functools.partial(_paged_kernel, n_chunks=n_chunks, chunk_pages=chunk_pages, page=page, steps=steps),
        grid_spec=grid_spec,
        out_shape=jax.ShapeDtypeStruct((bsz, n_rows, KV_LORA), F32),
        compiler_params=_cparams("arbitrary"),
        name="paged_mla",
    )(page_table, ql, qr, c_new, kr_new, cache_latent, cache_k_rope_t)


def _pad_heads(w, width):
    k, h, _ = w.shape
    return jnp.pad(w, ((0, 0), (0, 0), (0, HEAD_PAD - width))).reshape(k, h * HEAD_PAD)


def _rope_swap(w):
    half = QK_ROPE // 2
    return jnp.concatenate([w[..., half:], w[..., :half]], axis=-1)


def _rope_tables(pos):
    half = QK_ROPE // 2
    freqs = ROPE_BASE ** (-jnp.arange(half, dtype=F32) / half)
    ang = pos.astype(F32)[:, None] * freqs
    cos, sin = jnp.cos(ang), jnp.sin(ang)
    n = pos.shape[0]
    tail = jnp.zeros((n, HEAD_PAD - QK_NOPE - QK_ROPE), F32)
    cos_t = jnp.concatenate([jnp.ones((n, QK_NOPE), F32), cos, cos, tail], axis=-1)
    sin_t = jnp.concatenate([jnp.zeros((n, QK_NOPE), F32), -sin, sin, tail], axis=-1)
    return cos_t, sin_t


def _prep_weights(p):
    bf = lambda a: a.astype(BF16)
    d = p["a_w_in"].shape[1]
    w = {}
    a_in = p["a_w_in"][0]
    d_rnn = p["a_conv_w"].shape[2]
    w["a_in"] = [bf(a_in[:, :d_rnn]), bf(a_in[:, d_rnn:2 * d_rnn]), bf(a_in[:, 2 * d_rnn:])]
    w["a_wri"] = bf(jnp.concatenate([p["a_gate_r_w"][0], p["a_gate_i_w"][0]], axis=-1))
    w["a_out"] = (bf(p["a_w_out"][0][:d_rnn]), bf(p["a_w_out"][0][d_rnn:]))
    b_in = p["b_w_in"][0]
    kv_a = p["w_kv_a"]
    zeros_n = jnp.zeros((d, QK_NOPE), F32)
    zeros_t = jnp.zeros((d, HEAD_PAD - QK_NOPE - QK_ROPE), F32)
    w_r = kv_a[:, KV_LORA:]
    uq = p["b_w_uq"][0].reshape(Q_LORA, MLA_HEADS, QK_NOPE + QK_ROPE)
    uq_n, uq_r = uq[..., :QK_NOPE], uq[..., QK_NOPE:]
    w["mla"] = {
        "g_mix": p["norm_mix"][1].reshape(1, d), "g_kv": p["kv_norm"].reshape(1, d),
        "w_q": bf(b_in[:, :Q_LORA]), "w_m": bf(b_in[:, Q_LORA:]),
        "w_c": bf(kv_a[:, :KV_LORA]),
        "w_r": bf(jnp.concatenate([zeros_n, w_r, zeros_t], axis=-1)),
        "w_rs": bf(jnp.concatenate([zeros_n, _rope_swap(w_r), zeros_t], axis=-1)),
        "g_lat": p["kv_latent_norm"].reshape(1, KV_LORA), "g_q": p["b_q_norm"][0].reshape(1, Q_LORA),
        "w_uq": bf(_pad_heads(jnp.concatenate([uq_n, uq_r], axis=-1), QK_NOPE + QK_ROPE)),
        "w_uqs": bf(_pad_heads(jnp.concatenate([jnp.zeros_like(uq_n), _rope_swap(uq_r)], axis=-1),
                               QK_NOPE + QK_ROPE)),
        "w_uk": bf(_pad_heads(p["w_uk"], QK_NOPE)),
        "w_uv": bf(p["w_uv"].reshape(KV_LORA, MLA_HEADS * V_DIM)),
    }
    w["uk_t"] = bf(jnp.transpose(p["w_uk"], (1, 2, 0)))
    w["uv_h"] = bf(jnp.transpose(p["w_uv"], (1, 0, 2)))
    hv = MLA_HEADS * V_DIM
    w["b_out"] = (bf(p["b_w_out"][0][:hv]), bf(p["b_w_out"][0][hv:]))
    w["peer"] = []
    for l in range(p["peer_wq"].shape[0]):
        wq_hi, wq_lo = _split_bf16(p["peer_wq"][l].T)
        sk_hi, sk_lo = _split_bf16(p["peer_sub_keys"][l])
        n_exp, dm = p["peer_u"][l].shape
        steps = n_exp // PEER_TE
        u_t = jnp.transpose(bf(p["peer_u"][l]).reshape(steps, PEER_TE, 2, dm // 2), (2, 0, 1, 3))
        vt_t = jnp.transpose(bf(p["peer_v"][l]).reshape(steps, 2, PEER_TE // 2, dm), (0, 1, 3, 2))
        w["peer"].append((wq_hi, wq_lo, sk_hi, sk_lo, u_t, vt_t))
    return w


def _run_group(x, pos, mem_k, mem_v, conv_state, h0, past, p, w):
    bsz, t, d = x.shape
    m = bsz * t
    xf = x.reshape(m, d)
    decode = past is not None
    tm = 512

    xb, gb, qm = norm_matmul(xf, p["norm_mix"][0], w["a_in"], (F32, F32, BF16), tm)
    c = xb.shape[1]
    xb3, gb3 = xb.reshape(bsz, t, c), gb.reshape(bsz, t, c)
    rg = (p["a_conv_w"][0], p["a_conv_b"][0], w["a_wri"], p["a_gate_r_b"][0], p["a_gate_i_b"][0], p["a_lambda"][0])
    if decode:
        ext = jnp.concatenate([jnp.swapaxes(conv_state, 0, 1), jnp.swapaxes(xb3, 0, 1)], axis=0)
        gated, h_last = rglru_step(ext, jnp.swapaxes(gb3, 0, 1), h0, *rg)
        gated = jnp.swapaxes(gated, 0, 1).reshape(m, c)
        new_conv = jnp.swapaxes(ext[t:], 0, 1)
    else:
        gated, h_last = rglru_seq(xb3, gb3, conv_state, h0, *rg, tt=512)
        gated = gated.reshape(m, c)
        h_last = h_last.reshape(bsz, c)
        new_conv = jnp.concatenate([conv_state, xb3], axis=1)[:, t:] if t < CONV_W - 1 else xb3[:, t - (CONV_W - 1):]
    n_mem = mem_k.shape[2]
    mk = mem_k.reshape(mem_k.shape[0], bsz, n_mem * MEM_HEADS, MEM_HD)
    mv = mem_v.reshape(mem_v.shape[0], bsz, n_mem * MEM_HEADS, MEM_HD)
    att = mem_attention(qm.reshape(bsz, t, MEM_W), mk, mv, 0, tt=512).reshape(m, MEM_W)
    xf = out_proj(xf, gated, att, *w["a_out"], tm)
    xf = peer_ffn(xf, p["norm_ffn"][0], *w["peer"][0], p["final_norm"], final_norm=False)

    cos_t, sin_t = _rope_tables(jnp.tile(pos, bsz))
    outs = mla_proj(xf, cos_t, sin_t, w["mla"], tm, with_kv=not decode)
    qm, c_kv, kr_pad, q_pad = outs[:4]
    k_rope = kr_pad[:, QK_NOPE:QK_NOPE + QK_ROPE]
    hv = MLA_HEADS * V_DIM
    if decode:
        cache_latent, cache_k_rope, page_table = past
        q4 = q_pad.reshape(m, MLA_HEADS, HEAD_PAD)
        q_nope = jnp.swapaxes(q4[..., :QK_NOPE], 0, 1)
        q_lat = head_matmul(q_nope, w["uk_t"], BF16)
        n_rows = t * MLA_HEADS
        q_lat = jnp.swapaxes(q_lat, 0, 1).reshape(bsz, n_rows, KV_LORA)
        q_rope = q4[..., QK_NOPE:QK_NOPE + QK_ROPE].reshape(bsz, n_rows, QK_ROPE)
        pad_rows = lambda a: jnp.pad(a.reshape(bsz, t, -1), ((0, 0), (0, NEW_ROWS - t), (0, 0)))
        o_lat = paged_mla(page_table, q_lat, q_rope, pad_rows(c_kv), pad_rows(k_rope),
                          cache_latent, jnp.swapaxes(cache_k_rope, 1, 2), steps=t, chunk_pages=PAGED_CHUNK_PAGES)
        o_lat = jnp.swapaxes(o_lat.reshape(m, MLA_HEADS, KV_LORA), 0, 1).astype(BF16)
        y_a = jnp.swapaxes(head_matmul(o_lat, w["uv_h"], BF16), 0, 1).reshape(m, hv)
    else:
        k_pad, v = outs[4:]
        hp = MLA_HEADS * HEAD_PAD
        y_a = flash_mla(q_pad.reshape(bsz, t, hp), k_pad.reshape(bsz, t, hp), v.reshape(bsz, t, hv), bq=1024)
        y_a = y_a.reshape(m, hv)
    att = mem_attention(qm.reshape(bsz, t, MEM_W), mk, mv, 1, tt=512).reshape(m, MEM_W)
    xf = out_proj(xf, y_a, att, *w["b_out"], tm)
    y = peer_ffn(xf, p["norm_ffn"][1], *w["peer"][1], p["final_norm"], final_norm=True)

    return (y.reshape(bsz, t, d), c_kv.reshape(bsz, t, KV_LORA), k_rope.reshape(bsz, t, QK_ROPE),
            new_conv[None], h_last[None])


def kernel(x_prompt, x_sample, cache_latent, cache_k_rope, state_conv, state_rglru, cache_mem_k, cache_mem_v,
           page_table, mem_prompt, norm_mix, norm_ffn, norm_mem, w_mem_kv, a_w_in, a_conv_w, a_conv_b,
           a_gate_r_w, a_gate_r_b, a_gate_i_w, a_gate_i_b, a_lambda, a_w_out, kv_norm, w_kv_a, kv_latent_norm,
           w_uk, w_uv, b_w_in, b_q_norm, b_w_uq, b_w_out, peer_wq, peer_sub_keys, peer_u, peer_v, final_norm):
    p = dict(norm_mix=norm_mix, norm_ffn=norm_ffn, a_w_in=a_w_in, a_conv_w=a_conv_w, a_conv_b=a_conv_b,
             a_gate_r_w=a_gate_r_w, a_gate_r_b=a_gate_r_b, a_gate_i_w=a_gate_i_w, a_gate_i_b=a_gate_i_b,
             a_lambda=a_lambda, a_w_out=a_w_out, kv_norm=kv_norm, w_kv_a=w_kv_a, kv_latent_norm=kv_latent_norm,
             w_uk=w_uk, w_uv=w_uv, b_w_in=b_w_in, b_q_norm=b_q_norm, b_w_uq=b_w_uq, b_w_out=b_w_out,
             peer_wq=peer_wq, peer_sub_keys=peer_sub_keys, peer_u=peer_u, peer_v=peer_v, final_norm=final_norm)
    w = _prep_weights(p)
    depth = norm_mem.shape[0]

    b, t, d = x_prompt.shape
    n_mem = mem_prompt.shape[1]
    mem_flat = mem_prompt.reshape(b * n_mem, d)
    kvs = [norm_matmul(mem_flat, norm_mem[l], [w_mem_kv[l].astype(BF16)], (F32,), 512)[0] for l in range(depth)]
    mem_k_prompt = jnp.stack([kv[:, :MEM_W].reshape(b, n_mem, MEM_HEADS, MEM_HD) for kv in kvs])
    mem_v_prompt = jnp.stack([kv[:, MEM_W:].reshape(b, n_mem, MEM_HEADS, MEM_HD) for kv in kvs])
    d_rnn = a_conv_w.shape[2]
    conv0 = jnp.zeros((b, CONV_W - 1, d_rnn), x_prompt.dtype)
    h0 = jnp.zeros((b, d_rnn), x_prompt.dtype)
    y_prompt, latent_prompt, k_rope_prompt, conv_prompt, rglru_prompt = _run_group(
        x_prompt, jnp.arange(t), mem_k_prompt, mem_v_prompt, conv0, h0, None, p, w)

    ts = x_sample.shape[1]
    past_len = page_table.shape[1] * cache_latent.shape[1]
    y_sample, latent_sample, k_rope_sample, conv_sample, rglru_sample = _run_group(
        x_sample, past_len + jnp.arange(ts), cache_mem_k, cache_mem_v, state_conv[0], state_rglru[0],
        (cache_latent, cache_k_rope, page_table), p, w)

    return (y_prompt, y_sample, latent_prompt, k_rope_prompt, conv_prompt, rglru_prompt,
            mem_k_prompt, mem_v_prompt, latent_sample, k_rope_sample, conv_sample, rglru_sample)
```
